```python
import math
import jax, jax.numpy as jnp
from jax import lax
import numpy as np

D_MODEL = 1024
BATCH = 4
SEQ = 8192
DEPTH = 1

GRID_W = 64
CTX_LEN = 256
DA_HEADS = 4
DA_QK = 64
DA_V = 2 * DA_QK
ML_HEADS = 4
ML_QK = 64
ML_V = 128
D_MIX = DA_HEADS * DA_V + ML_HEADS * ML_V
D_FF = 4 * D_MODEL
ROPE_BASE = 10000.0
Q_BLOCK = 128
ML_CHUNK = 64
EPS = 1e-6

DA_Q_COLS = DA_HEADS * 2 * DA_QK
DA_K_COLS = DA_HEADS * 2 * DA_QK
DA_V_COLS = DA_HEADS * DA_V
ML_Q_COLS = ML_HEADS * ML_QK
ML_K_COLS = ML_HEADS * ML_QK
ML_V_COLS = ML_HEADS * ML_V
ML_O_COLS = ML_HEADS * ML_V
ML_G_COLS = 4 * ML_HEADS
IN_SPLITS = (DA_Q_COLS,
             DA_Q_COLS + DA_K_COLS,
             DA_Q_COLS + DA_K_COLS + DA_V_COLS,
             DA_Q_COLS + DA_K_COLS + DA_V_COLS + ML_Q_COLS,
             DA_Q_COLS + DA_K_COLS + DA_V_COLS + ML_Q_COLS + ML_K_COLS,
             DA_Q_COLS + DA_K_COLS + DA_V_COLS + ML_Q_COLS + ML_K_COLS + ML_V_COLS,
             DA_Q_COLS + DA_K_COLS + DA_V_COLS + ML_Q_COLS + ML_K_COLS + ML_V_COLS + ML_O_COLS)
IN_COLS = IN_SPLITS[-1] + ML_G_COLS

kernel_name = "hybrid_diffattn_mlstm_dit_block"


def rmsnorm(x, g):
    xf = x.astype(jnp.float32)
    y = xf * lax.rsqrt(jnp.mean(xf * xf, axis=-1, keepdims=True) + EPS)
    return (y * g.astype(jnp.float32)).astype(x.dtype)


def axial_rope(T):
    rows = T // GRID_W
    row = jnp.repeat(jnp.arange(rows, dtype=jnp.float32), GRID_W)
    col = jnp.tile(jnp.arange(GRID_W, dtype=jnp.float32), rows)
    half = DA_QK // 2
    inv = ROPE_BASE ** (-jnp.arange(0, half, 2, dtype=jnp.float32) / half)
    ar = row[:, None] * inv
    ac = col[:, None] * inv
    ang = jnp.concatenate([ar, ar, ac, ac], axis=-1)
    return jnp.cos(ang), jnp.sin(ang)


def apply_rope(x, cos, sin):
    def rot(u):
        u1, u2 = jnp.split(u, 2, axis=-1)
        return jnp.concatenate([-u2, u1], axis=-1)
    xr, xc = jnp.split(x, 2, axis=-1)
    xrot = jnp.concatenate([rot(xr), rot(xc)], axis=-1)
    return (x * cos + xrot * sin).astype(x.dtype)


def project(xm, w_in_l, b_gate_l):
    B_, T_ = xm.shape[0], xm.shape[1]
    h = xm @ w_in_l
    dq, dk, dv, mq, mk, mv, mo, mg = jnp.split(h, IN_SPLITS, axis=-1)
    dq = dq.reshape(B_, T_, DA_HEADS, 2, DA_QK).transpose(0, 2, 3, 1, 4)
    dk = dk.reshape(B_, T_, DA_HEADS, 2, DA_QK).transpose(0, 2, 3, 1, 4)
    dv = dv.reshape(B_, T_, DA_HEADS, DA_V).transpose(0, 2, 1, 3)
    mq = mq.reshape(B_, T_, ML_HEADS, ML_QK).transpose(0, 2, 1, 3)
    mk = mk.reshape(B_, T_, ML_HEADS, ML_QK).transpose(0, 2, 1, 3) * (ML_QK ** -0.5)
    mv = mv.reshape(B_, T_, ML_HEADS, ML_V).transpose(0, 2, 1, 3)
    mo = jax.nn.sigmoid(mo)
    mg = (mg.reshape(B_, T_, 4, ML_HEADS) + b_gate_l.reshape(4, ML_HEADS)).transpose(2, 0, 3, 1)
    return dq, dk, dv, mq, mk, mv, mo, mg


def diff_attn_core(q, k, v, lam):
    s = jnp.einsum('bhmqd,bhmkd->bhmqk', q, k).astype(jnp.float32) * (DA_QK ** -0.5)
    p = jax.nn.softmax(s, axis=-1)
    a = p[:, :, 0] - lam * p[:, :, 1]
    return jnp.einsum('bhqk,bhkv->bhqv', a.astype(v.dtype), v)


def diff_attn_latent(q, k_all, v_all, lam):
    B_, H_, _, T_, d = q.shape
    nb = T_ // Q_BLOCK
    qb = jnp.moveaxis(q.reshape(B_, H_, 2, nb, Q_BLOCK, d), 3, 0)
    out = lax.map(lambda qblk: diff_attn_core(qblk, k_all, v_all, lam), qb)
    return jnp.moveaxis(out, 0, 2).reshape(B_, H_, T_, v_all.shape[-1])


def mlstm_chunked(q, k, v, li, lf, state):
    out_dtype = v.dtype
    q, k, v = q.astype(jnp.float32), k.astype(jnp.float32), v.astype(jnp.float32)
    B_, H_, T_, dk = q.shape
    dv = v.shape[-1]
    nc = T_ // ML_CHUNK

    def chunks(a):
        a = a.reshape((B_, H_, nc, ML_CHUNK) + a.shape[3:])
        return jnp.moveaxis(a, 2, 0)

    tril = jnp.tril(jnp.ones((ML_CHUNK, ML_CHUNK), dtype=bool))

    def step(carry, inp):
        C, n, m = carry
        qc, kc, vc, lic, lfc = inp
        b = jnp.cumsum(lfc, axis=-1)
        dmat = b[..., :, None] - b[..., None, :] + lic[..., None, :]
        dmat = jnp.where(tril, dmat, -jnp.inf)
        inter = b + m[..., None]
        m_t = jnp.maximum(inter, jnp.max(dmat, axis=-1))
        s = jnp.einsum('bhtd,bhsd->bhts', qc, kc) * jnp.exp(dmat - m_t[..., None])
        w_inter = jnp.exp(inter - m_t)
        num = jnp.einsum('bhts,bhsv->bhtv', s, vc) + w_inter[..., None] * jnp.einsum('bhvd,bhtd->bhtv', C, qc)
        den = jnp.sum(s, axis=-1) + w_inter * jnp.einsum('bhd,bhtd->bht', n, qc)
        h = num / jnp.maximum(jnp.abs(den), jnp.exp(-m_t))[..., None]
        b_last = b[..., -1]
        g = b_last[..., None] - b + lic
        m_new = jnp.maximum(b_last + m, jnp.max(g, axis=-1))
        w = jnp.exp(g - m_new[..., None])
        decay = jnp.exp(b_last + m - m_new)
        C = decay[..., None, None] * C + jnp.einsum('bhsv,bhsd->bhvd', vc * w[..., None], kc)
        n = decay[..., None] * n + jnp.einsum('bhs,bhsd->bhd', w, kc)
        return (C, n, m_new), h

    state, h = lax.scan(step, state, (chunks(q), chunks(k), chunks(v), chunks(li), chunks(lf)))
    h = jnp.moveaxis(h, 0, 2).reshape(B_, H_, T_, dv)
    return h.astype(out_dtype), state


def mlstm_zero_state(B_):
    return (jnp.zeros((B_, ML_HEADS, ML_V, ML_QK), jnp.float32),
            jnp.zeros((B_, ML_HEADS, ML_QK), jnp.float32),
            jnp.zeros((B_, ML_HEADS), jnp.float32))


def gates(mg):
    mg = mg.astype(jnp.float32)
    return mg[0], jax.nn.log_sigmoid(mg[1]), mg[2], jax.nn.log_sigmoid(mg[3])


def merge_heads(da_h, ml_h, mo, subln_g_l, ml_norm_g_l, lambda_init, w_out_l):
    B_, _, T_, _ = da_h.shape
    da = rmsnorm(da_h, subln_g_l) * (1.0 - lambda_init)
    da = da.transpose(0, 2, 1, 3).reshape(B_, T_, DA_HEADS * DA_V)
    ml = rmsnorm(ml_h, ml_norm_g_l.reshape(ML_HEADS, 1, ML_V))
    ml = ml.transpose(0, 2, 1, 3).reshape(B_, T_, ML_HEADS * ML_V) * mo
    return jnp.concatenate([da, ml], axis=-1) @ w_out_l


def sq_relu_mlp(x, w1, w2):
    return jnp.square(jax.nn.relu(x @ w1)) @ w2


def hybrid_mixer(xm, cm, w_in_l, b_gate_l, lq1, lk1, lq2, lk2, subln_g_l, ml_norm_g_l, w_out_l,
                 cos, sin, lambda_init, update_ctx):
    B_ = xm.shape[0]
    dq, dk, dv, mq, mk, mv, mo, mg = project(xm, w_in_l, b_gate_l)
    cdq, cdk, cdv, cmq, cmk, cmv, cmo, cmg = project(cm, w_in_l, b_gate_l)
    lam = (jnp.exp(jnp.sum(lq1.astype(jnp.float32) * lk1.astype(jnp.float32)))
           - jnp.exp(jnp.sum(lq2.astype(jnp.float32) * lk2.astype(jnp.float32))) + lambda_init)

    q_lat = apply_rope(dq, cos, sin)
    k_all = jnp.concatenate([cdk, apply_rope(dk, cos, sin)], axis=3)
    v_all = jnp.concatenate([cdv, dv], axis=2)
    da_lat = diff_attn_latent(q_lat, k_all, v_all, lam)

    li_f, lf_f, li_b, lf_b = gates(mg)
    cli_f, clf_f, cli_b, clf_b = gates(cmg)
    flip = lambda a: jnp.flip(a, axis=2)
    zero = mlstm_zero_state(B_)
    hc_f, st_f = mlstm_chunked(cmq, cmk, cmv, cli_f, clf_f, zero)
    hc_b_rev, st_b = mlstm_chunked(flip(cmq), flip(cmk), flip(cmv), flip(cli_b), flip(clf_b), zero)
    h_f, _ = mlstm_chunked(mq, mk, mv, li_f, lf_f, st_f)
    h_b_rev, _ = mlstm_chunked(flip(mq), flip(mk), flip(mv), flip(li_b), flip(lf_b), st_b)
    ml_lat = h_f + flip(h_b_rev)

    y_lat = merge_heads(da_lat, ml_lat, mo, subln_g_l, ml_norm_g_l, lambda_init, w_out_l)
    if update_ctx:
        da_ctx = diff_attn_core(cdq, cdk, cdv, lam)
        ml_ctx = hc_f + flip(hc_b_rev)
        y_ctx = merge_heads(da_ctx, ml_ctx, cmo, subln_g_l, ml_norm_g_l, lambda_init, w_out_l)
        return y_lat, y_ctx
    return y_lat, None


def setup_inputs(seed: int = 0) -> dict:
    key = jax.random.key(seed)
    ks = jax.random.split(key, 20)
    f32 = jnp.float32
    nrm = lambda k, shape, s: jax.random.normal(k, shape, f32) * s
    f_bias = jnp.tile(jnp.linspace(3.0, 6.0, ML_HEADS, dtype=f32), 2)
    gate_offset = jnp.concatenate([jnp.zeros((ML_HEADS,), f32), f_bias[:ML_HEADS],
                                   jnp.zeros((ML_HEADS,), f32), f_bias[ML_HEADS:]])
    return {
        "x": nrm(ks[0], (BATCH, SEQ, D_MODEL), 1.0),
        "c": nrm(ks[1], (BATCH, D_MODEL), 1.0),
        "ctx": nrm(ks[2], (BATCH, CTX_LEN, D_MODEL), 1.0),
        "c_ctx": nrm(ks[3], (D_MODEL,), 1.0),
        "w_ada": nrm(ks[4], (DEPTH, D_MODEL, 6 * D_MODEL), D_MODEL ** -0.5),
        "b_ada": nrm(ks[5], (DEPTH, 6 * D_MODEL), 0.02),
        "norm1_g": 1.0 + nrm(ks[6], (DEPTH, D_MODEL), 0.02),
        "norm2_g": 1.0 + nrm(ks[7], (DEPTH, D_MODEL), 0.02),
        "w_in": nrm(ks[8], (DEPTH, D_MODEL, IN_COLS), D_MODEL ** -0.5),
        "b_gate": gate_offset + nrm(ks[9], (DEPTH, ML_G_COLS), 0.1),
        "lam_q1": nrm(ks[10], (DEPTH, DA_QK), 0.1),
        "lam_k1": nrm(ks[11], (DEPTH, DA_QK), 0.1),
        "lam_q2": nrm(ks[12], (DEPTH, DA_QK), 0.1),
        "lam_k2": nrm(ks[13], (DEPTH, DA_QK), 0.1),
        "subln_g": 1.0 + nrm(ks[14], (DEPTH, DA_V), 0.02),
        "mlstm_norm_g": 1.0 + nrm(ks[15], (DEPTH, ML_HEADS * ML_V), 0.02),
        "w_out": nrm(ks[16], (DEPTH, D_MIX, D_MODEL), D_MIX ** -0.5),
        "w_fc1": nrm(ks[17], (DEPTH, D_MODEL, D_FF), D_MODEL ** -0.5),
        "w_fc2": nrm(ks[18], (DEPTH, D_FF, D_MODEL), D_FF ** -0.5),
        "final_g": 1.0 + nrm(ks[19], (D_MODEL,), 0.02),
    }


def reference(x, c, ctx, c_ctx, w_ada, b_ada, norm1_g, norm2_g, w_in, b_gate, lam_q1, lam_k1, lam_q2, lam_k2,
              subln_g, mlstm_norm_g, w_out, w_fc1, w_fc2, final_g):
    T = x.shape[1]
    cos, sin = axial_rope(T)
    for l in range(DEPTH):
        update_ctx = l < DEPTH - 1
        lambda_init = 0.8 - 0.6 * math.exp(-0.3 * l)
        mod = jax.nn.silu(c) @ w_ada[l] + b_ada[l]
        mod_c = jax.nn.silu(c_ctx) @ w_ada[l] + b_ada[l]
        sh1, sc1, g1, sh2, sc2, g2 = jnp.split(mod[:, None, :], 6, axis=-1)
        csh1, csc1, cg1, csh2, csc2, cg2 = jnp.split(mod_c[None, None, :], 6, axis=-1)
        xm = rmsnorm(x, norm1_g[l]) * (1.0 + sc1) + sh1
        cm = rmsnorm(ctx, norm1_g[l]) * (1.0 + csc1) + csh1
        y, y_ctx = hybrid_mixer(xm, cm, w_in[l], b_gate[l], lam_q1[l], lam_k1[l], lam_q2[l], lam_k2[l],
                                subln_g[l], mlstm_norm_g[l], w_out[l], cos, sin, lambda_init, update_ctx)
        x = x + g1 * y
        x = x + g2 * sq_relu_mlp(rmsnorm(x, norm2_g[l]) * (1.0 + sc2) + sh2, w_fc1[l], w_fc2[l])
        if update_ctx:
            ctx = ctx + cg1 * y_ctx
            ctx = ctx + cg2 * sq_relu_mlp(rmsnorm(ctx, norm2_g[l]) * (1.0 + csc2) + csh2, w_fc1[l], w_fc2[l])
    return rmsnorm(x, final_g)
```

```python
import functools

import jax
import jax.numpy as jnp
from jax import lax
from jax.experimental import pallas as pl
from jax.experimental.pallas import tpu as pltpu

DA_HEADS = 4
DA_QK = 64
DA_V = 128
ML_HEADS = 4
ML_QK = 64
ML_V = 128
GRID_W = 64
ROPE_BASE = 10000.0
EPS = 1e-6

TOK_TILE = 256
ATT_TQ = 256
ATT_TK = 768
ML_CHUNK_LEN = 256
ML_STATE_ROWS = 144
VMEM_LIMIT_BYTES = 56 * 1024 * 1024

F32 = jnp.float32
BF16 = jnp.bfloat16


def _rms(x, g):
    return x * lax.rsqrt(jnp.mean(x * x, axis=-1, keepdims=True) + EPS) * g


def _adaln_kernel(c_ref, w_ref, b_ref, o_ref):
    c = c_ref[...]
    s = (c * jax.nn.sigmoid(c)).astype(BF16)
    o_ref[...] = jnp.dot(s, w_ref[...].astype(BF16), preferred_element_type=F32) + b_ref[...]


def _adaln(cc, w, b):
    rows, d = cc.shape
    n = w.shape[1]
    bn = d
    return pl.pallas_call(
        _adaln_kernel,
        grid=(n // bn,),
        in_specs=[pl.BlockSpec((rows, d), lambda j: (0, 0)),
                  pl.BlockSpec((d, bn), lambda j: (0, j)),
                  pl.BlockSpec((1, bn), lambda j: (0, j))],
        out_specs=pl.BlockSpec((rows, bn), lambda j: (0, j)),
        out_shape=jax.ShapeDtypeStruct((rows, n), F32),
        name="adaln",
    )(cc, w, b)


def _proj_kernel(x_ref, ctx_ref, sc_ref, sh_ref, csc_ref, csh_ref, g_ref, wn_ref, wt_ref, bg_ref,
                 cosn_ref, sinn_ref, cost_ref, sint_ref,
                 ka_ref, km_ref, mo_ref, qat_ref, vat_ref, qmt_ref, vmt_ref, gt_ref, *, nct):
    i = pl.program_id(1)
    is_ctx = i < nct
    x = jnp.where(is_ctx, ctx_ref[0], x_ref[0])
    sc = jnp.where(is_ctx, csc_ref[0], sc_ref[0])
    sh = jnp.where(is_ctx, csh_ref[0], sh_ref[0])
    xm = _rms(x, g_ref[...]) * (1.0 + sc) + sh
    xb = xm.astype(BF16)
    hn = jnp.dot(xb, wn_ref[...], preferred_element_type=F32)
    ht = lax.dot_general(wt_ref[...], xb, (((1,), (1,)), ((), ())),
                         preferred_element_type=F32)

    cosn, sinn = cosn_ref[...], sinn_ref[...]
    cost, sint = cost_ref[...], sint_ref[...]
    hd = DA_HEADS * 2 * DA_QK
    for h in range(DA_HEADS):
        lo, hi = h * 128, (h + 1) * 128
        ka_ref[0, :, lo:hi] = (hn[:, lo:hi] * cosn + hn[:, hd + lo:hd + hi] * sinn).astype(BF16)
        qat_ref[0, lo:hi, :] = (ht[lo:hi] * cost + ht[hd + lo:hd + hi] * sint).astype(BF16)
    km_ref[0] = hn[:, 2 * hd:2 * hd + 256].astype(BF16)
    mo_ref[0] = jax.nn.sigmoid(hn[:, 2 * hd + 256:2 * hd + 768]).astype(BF16)

    vat_ref[0, 0] = ht[2 * hd:3 * hd].astype(BF16)
    qmt_ref[0] = ht[3 * hd:3 * hd + 256].astype(BF16)
    vmt_ref[0] = ht[3 * hd + 256:3 * hd + 768].astype(BF16)
    g = ht[3 * hd + 768:3 * hd + 784] + bg_ref[...]
    row = lax.broadcasted_iota(jnp.int32, g.shape, 0)
    logsig = jnp.minimum(g, 0.0) - jnp.log1p(jnp.exp(-jnp.abs(g)))
    gt_ref[0] = jnp.where(row >= 8, logsig, g)


def _project(x, ctx, sc, sh, csc, csh, g, wn, wt, bg, cosn, sinn, cost, sint):
    bsz, t, d = x.shape
    nctx = ctx.shape[1]
    n = nctx + t
    tm = TOK_TILE
    nct = nctx // tm
    r = ATT_TK // tm
    hd = DA_HEADS * 2 * DA_QK
    const2 = lambda b, i: (0, 0)
    const3 = lambda b, i: (0, 0, 0)
    in_specs = [
        pl.BlockSpec((1, tm, d), lambda b, i: (b, jnp.maximum(i - nct, 0), 0)),
        pl.BlockSpec((1, tm, d), lambda b, i: (b, jnp.minimum(i, nct - 1), 0)),
        pl.BlockSpec((1, 1, d), lambda b, i: (b, 0, 0)),
        pl.BlockSpec((1, 1, d), lambda b, i: (b, 0, 0)),
        pl.BlockSpec((1, 1, d), const3),
        pl.BlockSpec((1, 1, d), const3),
        pl.BlockSpec((1, d), const2),
        pl.BlockSpec(wn.shape, const2),
        pl.BlockSpec(wt.shape, const2),
        pl.BlockSpec(bg.shape, const2),
        pl.BlockSpec((tm, 128), lambda b, i: (i, 0)),
        pl.BlockSpec((tm, 128), lambda b, i: (i, 0)),
        pl.BlockSpec((128, tm), lambda b, i: (0, i)),
        pl.BlockSpec((128, tm), lambda b, i: (0, i)),
    ]
    out_specs = [
        pl.BlockSpec((1, tm, hd), lambda b, i: (b, i, 0)),
        pl.BlockSpec((1, tm, 256), lambda b, i: (b, i, 0)),
        pl.BlockSpec((1, tm, 512), lambda b, i: (b, i, 0)),
        pl.BlockSpec((1, hd, tm), lambda b, i: (b, 0, i)),
        pl.BlockSpec((1, 1, 512, tm), lambda b, i: (b, i // r, 0, i % r)),
        pl.BlockSpec((1, 256, tm), lambda b, i: (b, 0, i)),
        pl.BlockSpec((1, 512, tm), lambda b, i: (b, 0, i)),
        pl.BlockSpec((1, 16, tm), lambda b, i: (b, 0, i)),
    ]
    out_shape = [
        jax.ShapeDtypeStruct((bsz, n, hd), BF16),
        jax.ShapeDtypeStruct((bsz, n, 256), BF16),
        jax.ShapeDtypeStruct((bsz, n, 512), BF16),
        jax.ShapeDtypeStruct((bsz, hd, n), BF16),
        jax.ShapeDtypeStruct((bsz, n // ATT_TK, 512, ATT_TK), BF16),
        jax.ShapeDtypeStruct((bsz, 256, n), BF16),
        jax.ShapeDtypeStruct((bsz, 512, n), BF16),
        jax.ShapeDtypeStruct((bsz, 16, n), F32),
    ]
    return pl.pallas_call(
        functools.partial(_proj_kernel, nct=nct),
        grid=(bsz, n // tm),
        in_specs=in_specs,
        out_specs=out_specs,
        out_shape=out_shape,
        compiler_params=pltpu.CompilerParams(
            dimension_semantics=("arbitrary", "arbitrary"), vmem_limit_bytes=VMEM_LIMIT_BYTES),
        name="project",
    )(x, ctx, sc, sh, csc, csh, g, wn, wt, bg, cosn, sinn, cost, sint)


def _attn_kernel(q_ref, k_ref, v_ref, lam_ref, g_ref, o_ref, m_s, l_s, acc_s, *, tq, tk, nblk, lambda_init):
    q = q_ref[0]
    row = lax.broadcasted_iota(jnp.int32, q.shape, 0)
    zero = jnp.zeros_like(q)
    qbd = jnp.concatenate([jnp.where(row < DA_QK, q, zero), jnp.where(row >= DA_QK, q, zero)], axis=1)

    m_s[...] = jnp.full(m_s.shape, -jnp.inf, F32)
    l_s[...] = jnp.zeros(l_s.shape, F32)
    acc_s[...] = jnp.zeros(acc_s.shape, F32)

    def body(j, carry):
        k = k_ref[0, pl.ds(pl.multiple_of(j * tk, tk), tk), :]
        v = v_ref[0, j]
        s = jnp.dot(k, qbd, preferred_element_type=F32)
        m_old = m_s[...]
        m_new = jnp.maximum(m_old, jnp.max(s, axis=0, keepdims=True))
        alpha = jnp.exp(m_old - m_new)
        p = jnp.exp(s - m_new)
        l_s[...] = alpha * l_s[...] + jnp.sum(p, axis=0, keepdims=True)
        acc_s[...] = alpha * acc_s[...] + jnp.dot(v, p.astype(BF16), preferred_element_type=F32)
        m_s[...] = m_new
        return carry

    lax.fori_loop(0, nblk, body, 0)

    lv = lam_ref[...]
    lam = (jnp.exp(jnp.sum(lv[0:1] * lv[1:2], axis=1, keepdims=True))
           - jnp.exp(jnp.sum(lv[2:3] * lv[3:4], axis=1, keepdims=True)) + lambda_init)
    inv_l = 1.0 / l_s[...]
    acc = acc_s[...]
    o = acc[:, :tq] * inv_l[:, :tq] - lam * (acc[:, tq:] * inv_l[:, tq:])
    y = o * lax.rsqrt(jnp.mean(o * o, axis=0, keepdims=True) + EPS) * g_ref[...] * (1.0 - lambda_init)
    o_ref[0] = y.T.astype(BF16)


def _attention(qat, ka, vat, lamv, subg, t, lambda_init):
    bsz, hd, n = qat.shape
    nctx = n - t
    tq, tk = ATT_TQ, ATT_TK
    nblk = n // tk
    qoff = nctx // tq
    kern = functools.partial(_attn_kernel, tq=tq, tk=tk, nblk=nblk, lambda_init=lambda_init)
    return pl.pallas_call(
        kern,
        grid=(bsz, DA_HEADS, t // tq),
        in_specs=[
            pl.BlockSpec((1, 128, tq), lambda b, h, i: (b, h, i + qoff)),
            pl.BlockSpec((1, n, 128), lambda b, h, i: (b, 0, h)),
            pl.BlockSpec((1, nblk, 128, tk), lambda b, h, i: (b, 0, h, 0)),
            pl.BlockSpec(lamv.shape, lambda b, h, i: (0, 0)),
            pl.BlockSpec(subg.shape, lambda b, h, i: (0, 0)),
        ],
        out_specs=pl.BlockSpec((1, tq, 128), lambda b, h, i: (b, i, h)),
        out_shape=jax.ShapeDtypeStruct((bsz, t, DA_HEADS * DA_V), BF16),
        scratch_shapes=[pltpu.VMEM((1, 2 * tq), F32), pltpu.VMEM((1, 2 * tq), F32),
                        pltpu.VMEM((DA_V, 2 * tq), F32)],
        compiler_params=pltpu.CompilerParams(
            dimension_semantics=("arbitrary", "arbitrary", "arbitrary"), vmem_limit_bytes=VMEM_LIMIT_BYTES),
        name="diff_attn",
    )(qat, ka, vat, lamv, subg)


def _mlstm_kernel(qf_ref, kf_ref, vf_ref, gf_ref, qb_ref, kb_ref, vb_ref, gb_ref,
                  hf_ref, hb_ref, c_s, m_s, *, L):
    j = pl.program_id(1)

    @pl.when(j == 0)
    def _():
        c_s[...] = jnp.zeros(c_s.shape, F32)
        m_s[...] = jnp.zeros(m_s.shape, F32)

    sidx = lax.broadcasted_iota(jnp.int32, (L, L), 0)
    tidx = lax.broadcasted_iota(jnp.int32, (L, L), 1)
    fwd_valid = sidx <= tidx
    bwd_valid = sidx >= tidx
    u_f = fwd_valid.astype(F32)
    u_b = bwd_valid.astype(F32)

    gf = gf_ref[0]
    gb = gb_ref[0]
    bf8 = jnp.dot(gf[8:16], u_f, preferred_element_type=F32, precision=lax.Precision.HIGHEST)
    bb8 = jnp.dot(gb[8:16], u_b, preferred_element_type=F32, precision=lax.Precision.HIGHEST)
    row8 = lax.broadcasted_iota(jnp.int32, (8, L), 0)
    is_f = row8 < ML_HEADS
    b8 = jnp.where(is_f, bf8, bb8)
    c8 = jnp.where(is_f, gf[0:8], gb[0:8]) - b8
    c_cols = jnp.concatenate([c8, jnp.zeros((120, L), F32)], axis=0).T

    row128 = lax.broadcasted_iota(jnp.int32, (128, L), 0)
    lane128 = lax.broadcasted_iota(jnp.int32, (ML_STATE_ROWS, 128), 1)
    row16 = lax.broadcasted_iota(jnp.int32, (16, L), 0)

    for r in range(2 * ML_HEADS):
        rev = r >= ML_HEADS
        h = r % ML_HEADS
        q_ref, k_ref, v_ref = (qb_ref, kb_ref, vb_ref) if rev else (qf_ref, kf_ref, vf_ref)
        out_ref = hb_ref if rev else hf_ref
        valid = bwd_valid if rev else fwd_valid
        pair, odd = h // 2, h % 2
        qpair = q_ref[0, pair * 128:(pair + 1) * 128, :]
        kpair = k_ref[0, :, pair * 128:(pair + 1) * 128]
        vt = v_ref[0, h * ML_V:(h + 1) * ML_V, :]
        own_rows = (row128 >= ML_QK) if odd else (row128 < ML_QK)
        own_lanes = (lane128 >= ML_QK) if odd else (lane128 < ML_QK)
        qt = jnp.where(own_rows, qpair, jnp.zeros_like(qpair))

        b_row = b8[r:r + 1]
        c_row = c8[r:r + 1]
        c_col = c_cols[:, r:r + 1]
        m_prev = m_s[r]
        state = c_s[r]

        dm = jnp.where(valid, c_col + b_row, -jnp.inf)
        inter = b_row + m_prev
        m_t = jnp.maximum(inter, jnp.max(dm, axis=0, keepdims=True))
        st = jnp.dot(kpair, qt, preferred_element_type=F32)
        sp = st * jnp.exp(dm - m_t)
        den_intra = jnp.sum(sp, axis=0, keepdims=True)
        num_intra = jnp.dot(vt, sp.astype(BF16), preferred_element_type=F32)
        cq = jnp.dot(state.astype(BF16), qpair, preferred_element_type=F32)
        w_inter = jnp.exp(inter - m_t)
        num = num_intra + w_inter * cq[0:ML_V]
        den = den_intra + w_inter * cq[ML_V:ML_V + 1]
        ht = num / jnp.maximum(jnp.abs(den), jnp.exp(-m_t))
        out_ref[0, :, h * ML_V:(h + 1) * ML_V] = ht.T

        b_tot = b_row[:, 0:1] if rev else b_row[:, L - 1:L]
        g_row = b_tot + c_row
        carry = b_tot + m_prev[:, 0:1]
        m_new = jnp.maximum(carry, jnp.max(g_row, axis=1, keepdims=True))
        w_row = jnp.exp(g_row - m_new)
        decay = jnp.exp(carry - m_new)
        aug = jnp.concatenate([vt.astype(F32) * w_row,
                               jnp.where(row16 == 0, jnp.broadcast_to(w_row, (16, L)), 0.0)], axis=0)
        upd = jnp.dot(aug.astype(BF16), kpair, preferred_element_type=F32)
        c_s[r] = decay * state + jnp.where(own_lanes, upd, 0.0)
        m_s[r] = jnp.broadcast_to(m_new, (1, L))


def _mlstm(qmt, km, vmt, gt, t):
    bsz, _, n = qmt.shape
    L = ML_CHUNK_LEN
    nch = n // L
    nctx_ch = (n - t) // L
    nlat = t // L

    def fwd_c(j):
        return j

    def bwd_c(j):
        return jnp.where(j < nctx_ch, nctx_ch - 1 - j, nch - 1 - (j - nctx_ch))

    def specs(cfn):
        return [
            pl.BlockSpec((1, 256, L), lambda b, j: (b, 0, cfn(j))),
            pl.BlockSpec((1, L, 256), lambda b, j: (b, cfn(j), 0)),
            pl.BlockSpec((1, 512, L), lambda b, j: (b, 0, cfn(j))),
            pl.BlockSpec((1, 16, L), lambda b, j: (b, 0, cfn(j))),
        ]

    lat = lambda j: jnp.maximum(j - nctx_ch, 0)
    out_specs = [
        pl.BlockSpec((1, L, 512), lambda b, j: (b, lat(j), 0)),
        pl.BlockSpec((1, L, 512), lambda b, j: (b, nlat - 1 - lat(j), 0)),
    ]
    out_shape = [jax.ShapeDtypeStruct((bsz, t, 512), F32)] * 2
    return pl.pallas_call(
        functools.partial(_mlstm_kernel, L=L),
        grid=(bsz, nch),
        in_specs=specs(fwd_c) + specs(bwd_c),
        out_specs=out_specs,
        out_shape=out_shape,
        scratch_shapes=[pltpu.VMEM((2 * ML_HEADS, ML_STATE_ROWS, 128), F32),
                        pltpu.VMEM((2 * ML_HEADS, 1, L), F32)],
        compiler_params=pltpu.CompilerParams(
            dimension_semantics=("arbitrary", "arbitrary"), vmem_limit_bytes=VMEM_LIMIT_BYTES),
        name="mlstm",
    )(qmt, km, vmt, gt, qmt, km, vmt, gt)


def _final_kernel(x_ref, da_ref, hf_ref, hb_ref, mo_ref, g1_ref, sh2_ref, sc2_ref, g2_ref,
                  mlg_ref, n2g_ref, fg_ref, wo_ref, w1_ref, w2_ref, o_ref):
    x = x_ref[0]
    hsum = hf_ref[0] + hb_ref[0]
    parts = []
    for h in range(ML_HEADS):
        u = hsum[:, h * ML_V:(h + 1) * ML_V]
        parts.append(u * lax.rsqrt(jnp.mean(u * u, axis=1, keepdims=True) + EPS))
    ml = jnp.concatenate(parts, axis=1) * mlg_ref[...] * mo_ref[0].astype(F32)
    cat = jnp.concatenate([da_ref[0], ml.astype(BF16)], axis=1)
    y = jnp.dot(cat, wo_ref[...], preferred_element_type=F32)
    x1 = x + g1_ref[0] * y
    xn = (_rms(x1, n2g_ref[...]) * (1.0 + sc2_ref[0]) + sh2_ref[0]).astype(BF16)
    hid = jnp.dot(xn, w1_ref[...], preferred_element_type=F32)
    hid = jnp.square(jnp.maximum(hid, 0.0)).astype(BF16)
    x2 = x1 + g2_ref[0] * jnp.dot(hid, w2_ref[...], preferred_element_type=F32)
    o_ref[0] = _rms(x2, fg_ref[...])


def _final(x, da, hf, hb, mo, g1, sh2, sc2, g2, mlg, n2g, fg, wo, w1, w2):
    bsz, t, d = x.shape
    tm = TOK_TILE
    moff = (mo.shape[1] - t) // tm
    tok = lambda w: pl.BlockSpec((1, tm, w), lambda b, i: (b, i, 0))
    mod = pl.BlockSpec((1, 1, d), lambda b, i: (b, 0, 0))
    const = lambda a: pl.BlockSpec(a.shape, lambda b, i: (0, 0), pipeline_mode=pl.Buffered(1))
    return pl.pallas_call(
        _final_kernel,
        grid=(bsz, t // tm),
        in_specs=[tok(d), tok(512), tok(512), tok(512),
                  pl.BlockSpec((1, tm, 512), lambda b, i: (b, i + moff, 0)),
                  mod, mod, mod, mod, const(mlg), const(n2g), const(fg), const(wo), const(w1), const(w2)],
        out_specs=tok(d),
        out_shape=jax.ShapeDtypeStruct((bsz, t, d), F32),
        compiler_params=pltpu.CompilerParams(
            dimension_semantics=("arbitrary", "arbitrary"), vmem_limit_bytes=VMEM_LIMIT_BYTES),
        name="final",
    )(x, da, hf, hb, mo, g1, sh2, sc2, g2, mlg, n2g, fg, wo, w1, w2)


def _rope_tables(t, nctx):
    rows = t // GRID_W
    row = jnp.repeat(jnp.arange(rows, dtype=F32), GRID_W)
    col = jnp.tile(jnp.arange(GRID_W, dtype=F32), rows)
    half = DA_QK // 2
    inv = ROPE_BASE ** (-jnp.arange(0, half, 2, dtype=F32) / half)
    ar = row[:, None] * inv
    ac = col[:, None] * inv
    ang = jnp.concatenate([ar, ar, ac, ac], axis=-1)
    cos = jnp.concatenate([jnp.ones((nctx, DA_QK), F32), jnp.cos(ang)], axis=0)
    sin = jnp.concatenate([jnp.zeros((nctx, DA_QK), F32), jnp.sin(ang)], axis=0)
    cos = jnp.tile(cos, (1, 2))
    sin = jnp.tile(sin, (1, 2))
    return cos, sin, cos.T, sin.T


def _rot_cols(w):
    k, n = w.shape
    w4 = w.reshape(k, n // 64, 2, 2, 16)
    rot = jnp.stack([-w4[:, :, :, 1], w4[:, :, :, 0]], axis=3)
    return rot.reshape(k, n)


def _split_weights(w_in_l, b_gate_l):
    nq = DA_HEADS * 2 * DA_QK
    o = 0
    wdq = w_in_l[:, o:o + nq]; o += nq
    wdk = w_in_l[:, o:o + nq]; o += nq
    wdv = w_in_l[:, o:o + DA_HEADS * DA_V]; o += DA_HEADS * DA_V
    wmq = w_in_l[:, o:o + ML_HEADS * ML_QK]; o += ML_HEADS * ML_QK
    wmk = w_in_l[:, o:o + ML_HEADS * ML_QK]; o += ML_HEADS * ML_QK
    wmv = w_in_l[:, o:o + ML_HEADS * ML_V]; o += ML_HEADS * ML_V
    wmo = w_in_l[:, o:o + ML_HEADS * ML_V]; o += ML_HEADS * ML_V
    wmg = w_in_l[:, o:o + 4 * ML_HEADS]
    perm = jnp.array([0, 1, 2, 3, 8, 9, 10, 11, 4, 5, 6, 7, 12, 13, 14, 15])
    wmg = wmg[:, perm]
    bg = b_gate_l[perm].reshape(16, 1).astype(F32)
    qscale = DA_QK ** -0.5
    kscale = ML_QK ** -0.5
    wn = jnp.concatenate([wdk, _rot_cols(wdk), wmk * kscale, wmo], axis=1).astype(BF16)
    wt = jnp.concatenate([wdq * qscale, _rot_cols(wdq) * qscale, wdv, wmq, wmv, wmg], axis=1).T.astype(BF16)
    return wn, wt, bg


def kernel(x, c, ctx, c_ctx, w_ada, b_ada, norm1_g, norm2_g, w_in, b_gate, lam_q1, lam_k1, lam_q2, lam_k2,
           subln_g, mlstm_norm_g, w_out, w_fc1, w_fc2, final_g):
    bsz, t, d = x.shape
    nctx = ctx.shape[1]
    depth = w_ada.shape[0]
    assert depth == 1, "single-layer block: the context stream is never updated"
    assert nctx % TOK_TILE == 0 and nctx % ML_CHUNK_LEN == 0 and nctx % ATT_TQ == 0
    assert t % ATT_TQ == 0 and t % ML_CHUNK_LEN == 0 and (nctx + t) % ATT_TK == 0 and t % GRID_W == 0
    lambda_init = 0.2

    cc = jnp.concatenate([c, c_ctx[None, :], jnp.zeros((8 - bsz - 1, d), F32)], axis=0)
    mod = _adaln(cc, w_ada[0], b_ada[0][None, :])
    mb = mod[:bsz].reshape(bsz, 1, 6, d)
    sh1, sc1, g1, sh2, sc2, g2 = [mb[:, :, k] for k in range(6)]
    mc = mod[bsz:bsz + 1].reshape(1, 1, 6, d)
    csh1, csc1 = mc[:, :, 0], mc[:, :, 1]

    wn, wt, bg = _split_weights(w_in[0], b_gate[0])
    cosn, sinn, cost, sint = _rope_tables(t, nctx)
    ka, km, mo, qat, vat, qmt, vmt, gt = _project(
        x, ctx, sc1, sh1, csc1, csh1, norm1_g[0][None, :], wn, wt, bg, cosn, sinn, cost, sint)

    lamv = jnp.stack([lam_q1[0], lam_k1[0], lam_q2[0], lam_k2[0]]).astype(F32)
    da = _attention(qat, ka, vat, lamv, subln_g[0].reshape(DA_V, 1), t, lambda_init)
    hf, hb = _mlstm(qmt, km, vmt, gt, t)

    return _final(x, da, hf, hb, mo, g1, sh2, sc2, g2, mlstm_norm_g[0][None, :], norm2_g[0][None, :],
                  final_g[None, :], w_out[0].astype(BF16), w_fc1[0].astype(BF16), w_fc2[0].astype(BF16))
```

```python
import functools

import jax
import jax.numpy as jnp
from jax import lax
from jax.experimental import pallas as pl
from jax.experimental.pallas import tpu as pltpu

DA_HEADS = 4
DA_QK = 64
DA_V = 128
ML_HEADS = 4
ML_QK = 64
ML_V = 128
GRID_W = 64
ROPE_BASE = 10000.0
EPS = 1e-6
LOG2_E = 1.4426950408889634

TOK_TILE = 256
ATT_TQ = 512
ATT_TK = 768
ML_CHUNK_LEN = 256
ML_STATE_ROWS = 144
VMEM_LIMIT_BYTES = 56 * 1024 * 1024

F32 = jnp.float32
BF16 = jnp.bfloat16


def _rms(x, g):
    return x * lax.rsqrt(jnp.mean(x * x, axis=-1, keepdims=True) + EPS) * g


def _adaln_kernel(c_ref, w_ref, b_ref, o_ref):
    c = c_ref[...]
    s = (c * jax.nn.sigmoid(c)).astype(BF16)
    o_ref[...] = jnp.dot(s, w_ref[...].astype(BF16), preferred_element_type=F32) + b_ref[...]


def _adaln(cc, w, b):
    rows, d = cc.shape
    n = w.shape[1]
    bn = d
    return pl.pallas_call(
        _adaln_kernel,
        grid=(n // bn,),
        in_specs=[pl.BlockSpec((rows, d), lambda j: (0, 0)),
                  pl.BlockSpec((d, bn), lambda j: (0, j)),
                  pl.BlockSpec((1, bn), lambda j: (0, j))],
        out_specs=pl.BlockSpec((rows, bn), lambda j: (0, j)),
        out_shape=jax.ShapeDtypeStruct((rows, n), F32),
        name="adaln",
    )(cc, w, b)


def _proj_kernel(x_ref, ctx_ref, sc_ref, sh_ref, csc_ref, csh_ref, g_ref, wn_ref, wt_ref, bg_ref,
                 cosn_ref, sinn_ref, cost_ref, sint_ref,
                 ka_ref, km_ref, mo_ref, qat_ref, vat_ref, qmt_ref, vmt_ref, gt_ref, *, nct):
    i = pl.program_id(1)
    is_ctx = i < nct
    x = jnp.where(is_ctx, ctx_ref[0], x_ref[0])
    sc = jnp.where(is_ctx, csc_ref[0], sc_ref[0])
    sh = jnp.where(is_ctx, csh_ref[0], sh_ref[0])
    xm = _rms(x, g_ref[...]) * (1.0 + sc) + sh
    xb = xm.astype(BF16)
    hn = jnp.dot(xb, wn_ref[...], preferred_element_type=F32)
    ht = lax.dot_general(wt_ref[...], xb, (((1,), (1,)), ((), ())),
                         preferred_element_type=F32)

    cosn, sinn = cosn_ref[...], sinn_ref[...]
    cost, sint = cost_ref[...], sint_ref[...]
    hd = DA_HEADS * 2 * DA_QK
    for h in range(DA_HEADS):
        lo, hi = h * 128, (h + 1) * 128
        ka_ref[0, :, lo:hi] = (hn[:, lo:hi] * cosn + hn[:, hd + lo:hd + hi] * sinn).astype(BF16)
        q = (ht[lo:hi] * cost + ht[hd + lo:hd + hi] * sint).astype(BF16)
        qrow = lax.broadcasted_iota(jnp.int32, q.shape, 0)
        zero = jnp.zeros_like(q)
        qat_ref[0, h, 0] = jnp.concatenate(
            [jnp.where(qrow < DA_QK, q, zero), jnp.where(qrow >= DA_QK, q, zero)], axis=1)
    km_ref[0] = hn[:, 2 * hd:2 * hd + 256].astype(BF16)
    mo_ref[0] = jax.nn.sigmoid(hn[:, 2 * hd + 256:2 * hd + 768]).astype(BF16)

    vat_ref[0, 0] = ht[2 * hd:3 * hd].astype(BF16)
    qmt_ref[0] = ht[3 * hd:3 * hd + 256].astype(BF16)
    vmt_ref[0] = ht[3 * hd + 256:3 * hd + 768].astype(BF16)
    g = ht[3 * hd + 768:3 * hd + 784] + bg_ref[...]
    row = lax.broadcasted_iota(jnp.int32, g.shape, 0)
    logsig = jnp.minimum(g, 0.0) - jnp.log1p(jnp.exp(-jnp.abs(g)))
    gt_ref[0] = jnp.where(row >= 8, logsig, g)


def _project(x, ctx, sc, sh, csc, csh, g, wn, wt, bg, cosn, sinn, cost, sint):
    bsz, t, d = x.shape
    nctx = ctx.shape[1]
    n = nctx + t
    tm = TOK_TILE
    nct = nctx // tm
    r = ATT_TK // tm
    hd = DA_HEADS * 2 * DA_QK
    const2 = lambda b, i: (0, 0)
    const3 = lambda b, i: (0, 0, 0)
    in_specs = [
        pl.BlockSpec((1, tm, d), lambda b, i: (b, jnp.maximum(i - nct, 0), 0)),
        pl.BlockSpec((1, tm, d), lambda b, i: (b, jnp.minimum(i, nct - 1), 0)),
        pl.BlockSpec((1, 1, d), lambda b, i: (b, 0, 0)),
        pl.BlockSpec((1, 1, d), lambda b, i: (b, 0, 0)),
        pl.BlockSpec((1, 1, d), const3),
        pl.BlockSpec((1, 1, d), const3),
        pl.BlockSpec((1, d), const2),
        pl.BlockSpec(wn.shape, const2),
        pl.BlockSpec(wt.shape, const2),
        pl.BlockSpec(bg.shape, const2),
        pl.BlockSpec((tm, 128), lambda b, i: (i, 0)),
        pl.BlockSpec((tm, 128), lambda b, i: (i, 0)),
        pl.BlockSpec((128, tm), lambda b, i: (0, i)),
        pl.BlockSpec((128, tm), lambda b, i: (0, i)),
    ]
    out_specs = [
        pl.BlockSpec((1, tm, hd), lambda b, i: (b, i, 0)),
        pl.BlockSpec((1, tm, 256), lambda b, i: (b, i, 0)),
        pl.BlockSpec((1, tm, 512), lambda b, i: (b, i, 0)),
        pl.BlockSpec((1, DA_HEADS, 1, 128, 2 * tm), lambda b, i: (b, 0, i, 0, 0)),
        pl.BlockSpec((1, 1, 512, tm), lambda b, i: (b, i // r, 0, i % r)),
        pl.BlockSpec((1, 256, tm), lambda b, i: (b, 0, i)),
        pl.BlockSpec((1, 512, tm), lambda b, i: (b, 0, i)),
        pl.BlockSpec((1, 16, tm), lambda b, i: (b, 0, i)),
    ]
    out_shape = [
        jax.ShapeDtypeStruct((bsz, n, hd), BF16),
        jax.ShapeDtypeStruct((bsz, n, 256), BF16),
        jax.ShapeDtypeStruct((bsz, n, 512), BF16),
        jax.ShapeDtypeStruct((bsz, DA_HEADS, n // tm, 128, 2 * tm), BF16),
        jax.ShapeDtypeStruct((bsz, n // ATT_TK, 512, ATT_TK), BF16),
        jax.ShapeDtypeStruct((bsz, 256, n), BF16),
        jax.ShapeDtypeStruct((bsz, 512, n), BF16),
        jax.ShapeDtypeStruct((bsz, 16, n), F32),
    ]
    return pl.pallas_call(
        functools.partial(_proj_kernel, nct=nct),
        grid=(bsz, n // tm),
        in_specs=in_specs,
        out_specs=out_specs,
        out_shape=out_shape,
        compiler_params=pltpu.CompilerParams(
            dimension_semantics=("arbitrary", "arbitrary"), vmem_limit_bytes=VMEM_LIMIT_BYTES),
        name="project",
    )(x, ctx, sc, sh, csc, csh, g, wn, wt, bg, cosn, sinn, cost, sint)


def _attn_kernel(q_ref, k_ref, v_ref, lam_ref, g_ref, o_ref, m_s, l_s, lfin_s, acc_s, s_s, mb_s, al_s, p_s, *,
                 tq, tk, nblk, ntile, qsub, qoff, lambda_init):
    nstep = ntile * nblk
    sw = q_ref.shape[-1]
    half = sw // 2

    m_s[...] = jnp.zeros(m_s.shape, F32)
    l_s[...] = jnp.zeros(l_s.shape, F32)
    lfin_s[...] = jnp.ones(lfin_s.shape, F32)
    acc_s[...] = jnp.zeros(acc_s.shape, F32)

    lv = lam_ref[...]
    lam = (jnp.exp(jnp.sum(lv[0:1] * lv[1:2], axis=1, keepdims=True))
           - jnp.exp(jnp.sum(lv[2:3] * lv[3:4], axis=1, keepdims=True)) + lambda_init)

    def values(u):
        kb = lax.rem(u, nblk)
        pv = jnp.dot(v_ref[0, kb], p_s[...], preferred_element_type=F32)
        acc_s[...] = al_s[...] * acc_s[...] + pv

    def scale(u):
        kb = lax.rem(u, nblk)
        first = kb == 0
        m_old = jnp.where(first, -jnp.inf, m_s[...])
        l_old = jnp.where(first, 0.0, l_s[...])
        m_new = jnp.maximum(m_old, mb_s[...])
        alpha = jnp.exp2(m_old - m_new)
        p = jnp.exp2(s_s[...] - m_new)
        l_new = alpha * l_old + jnp.sum(p, axis=0, keepdims=True)
        l_s[...] = l_new
        lfin_s[...] = jnp.where(kb == nblk - 1, l_new, lfin_s[...])
        p_s[...] = p.astype(BF16)
        al_s[...] = alpha
        m_s[...] = m_new

    def scores(u):
        tile = lax.div(u, nblk)
        kb = lax.rem(u, nblk)
        qbd = jnp.concatenate([q_ref[0, 0, qoff + tile * qsub + c] for c in range(qsub)], axis=1)
        k = k_ref[0, pl.ds(pl.multiple_of(kb * tk, tk), tk), :]
        s = jnp.dot(k, qbd, preferred_element_type=F32)
        s_s[...] = s
        mb_s[...] = jnp.max(s, axis=0, keepdims=True)

    def finalize(tile):
        inv_l = 1.0 / lfin_s[...]
        acc = acc_s[...]
        for c in range(qsub):
            a1, a2 = c * sw, c * sw + half
            o = acc[:, a1:a2] * inv_l[:, a1:a2] - lam * (acc[:, a2:a2 + half] * inv_l[:, a2:a2 + half])
            y = o * lax.rsqrt(jnp.mean(o * o, axis=0, keepdims=True) + EPS) * g_ref[...] * (1.0 - lambda_init)
            o_ref[0, pl.ds(pl.multiple_of(tile * tq + c * half, half), half), :] = y.T.astype(BF16)

    scores(0)
    scale(0)
    scores(1)

    def body(u, carry):
        values(u - 1)
        scale(u)
        scores(u + 1)

        @pl.when(lax.rem(u - 1, nblk) == nblk - 1)
        def _():
            finalize(lax.div(u - 1, nblk))
        return carry

    lax.fori_loop(1, nstep - 1, body, 0)
    values(nstep - 2)
    scale(nstep - 1)
    values(nstep - 1)
    finalize(ntile - 1)


def _attention(qat, ka, vat, lamv, subg, t, lambda_init):
    bsz, _, nsub, _, sw = qat.shape
    n = ka.shape[1]
    tq, tk = ATT_TQ, ATT_TK
    sub = sw // 2
    nblk = n // tk
    kern = functools.partial(_attn_kernel, tq=tq, tk=tk, nblk=nblk, ntile=t // tq, qsub=tq // sub,
                             qoff=(n - t) // sub, lambda_init=lambda_init)
    row = lambda: pltpu.VMEM((1, 2 * tq), F32)
    return pl.pallas_call(
        kern,
        grid=(bsz, DA_HEADS),
        in_specs=[
            pl.BlockSpec((1, 1, nsub, 128, sw), lambda b, h: (b, h, 0, 0, 0)),
            pl.BlockSpec((1, n, 128), lambda b, h: (b, 0, h)),
            pl.BlockSpec((1, nblk, 128, tk), lambda b, h: (b, 0, h, 0)),
            pl.BlockSpec(lamv.shape, lambda b, h: (0, 0)),
            pl.BlockSpec(subg.shape, lambda b, h: (0, 0)),
        ],
        out_specs=pl.BlockSpec((1, t, 128), lambda b, h: (b, 0, h)),
        out_shape=jax.ShapeDtypeStruct((bsz, t, DA_HEADS * DA_V), BF16),
        scratch_shapes=[row(), row(), row(), pltpu.VMEM((DA_V, 2 * tq), F32), pltpu.VMEM((tk, 2 * tq), F32),
                        row(), row(), pltpu.VMEM((tk, 2 * tq), BF16)],
        compiler_params=pltpu.CompilerParams(
            dimension_semantics=("arbitrary", "arbitrary"), vmem_limit_bytes=VMEM_LIMIT_BYTES),
        name="diff_attn",
    )(qat, ka, vat, lamv, subg)


def _mlstm_kernel(qf_ref, kf_ref, vf_ref, gf_ref, qb_ref, kb_ref, vb_ref, gb_ref,
                  hf_ref, hb_ref, c_s, m_s, *, L):
    j = pl.program_id(1)

    @pl.when(j == 0)
    def _():
        c_s[...] = jnp.zeros(c_s.shape, F32)
        m_s[...] = jnp.zeros(m_s.shape, F32)

    sidx = lax.broadcasted_iota(jnp.int32, (L, L), 0)
    tidx = lax.broadcasted_iota(jnp.int32, (L, L), 1)
    fwd_valid = sidx <= tidx
    bwd_valid = sidx >= tidx
    u_f = fwd_valid.astype(F32)
    u_b = bwd_valid.astype(F32)

    gf = gf_ref[0]
    gb = gb_ref[0]
    bf8 = jnp.dot(gf[8:16], u_f, preferred_element_type=F32, precision=lax.Precision.HIGHEST)
    bb8 = jnp.dot(gb[8:16], u_b, preferred_element_type=F32, precision=lax.Precision.HIGHEST)
    row8 = lax.broadcasted_iota(jnp.int32, (8, L), 0)
    is_f = row8 < ML_HEADS
    b8 = jnp.where(is_f, bf8, bb8)
    c8 = jnp.where(is_f, gf[0:8], gb[0:8]) - b8
    c_cols = jnp.concatenate([c8, jnp.zeros((120, L), F32)], axis=0).T

    row128 = lax.broadcasted_iota(jnp.int32, (128, L), 0)
    lane128 = lax.broadcasted_iota(jnp.int32, (ML_STATE_ROWS, 128), 1)
    row16 = lax.broadcasted_iota(jnp.int32, (16, L), 0)

    for r in range(2 * ML_HEADS):
        rev = r >= ML_HEADS
        h = r % ML_HEADS
        q_ref, k_ref, v_ref = (qb_ref, kb_ref, vb_ref) if rev else (qf_ref, kf_ref, vf_ref)
        out_ref = hb_ref if rev else hf_ref
        valid = bwd_valid if rev else fwd_valid
        pair, odd = h // 2, h % 2
        qpair = q_ref[0, pair * 128:(pair + 1) * 128, :]
        kpair = k_ref[0, :, pair * 128:(pair + 1) * 128]
        vt = v_ref[0, h * ML_V:(h + 1) * ML_V, :]
        own_rows = (row128 >= ML_QK) if odd else (row128 < ML_QK)
        own_lanes = (lane128 >= ML_QK) if odd else (lane128 < ML_QK)
        qt = jnp.where(own_rows, qpair, jnp.zeros_like(qpair))

        b_row = b8[r:r + 1]
        c_row = c8[r:r + 1]
        c_col = c_cols[:, r:r + 1]
        m_prev = m_s[r]
        state = c_s[r]

        dm = jnp.where(valid, c_col + b_row, -jnp.inf)
        inter = b_row + m_prev
        m_t = jnp.maximum(inter, jnp.max(dm, axis=0, keepdims=True))
        st = jnp.dot(kpair, qt, preferred_element_type=F32)
        sp = st * jnp.exp(dm - m_t)
        den_intra = jnp.sum(sp, axis=0, keepdims=True)
        num_intra = jnp.dot(vt, sp.astype(BF16), preferred_element_type=F32)
        cq = jnp.dot(state.astype(BF16), qpair, preferred_element_type=F32)
        w_inter = jnp.exp(inter - m_t)
        num = num_intra + w_inter * cq[0:ML_V]
        den = den_intra + w_inter * cq[ML_V:ML_V + 1]
        ht = num / jnp.maximum(jnp.abs(den), jnp.exp(-m_t))
        out_ref[0, :, h * ML_V:(h + 1) * ML_V] = ht.T

        b_tot = b_row[:, 0:1] if rev else b_row[:, L - 1:L]
        g_row = b_tot + c_row
        carry = b_tot + m_prev[:, 0:1]
        m_new = jnp.maximum(carry, jnp.max(g_row, axis=1, keepdims=True))
        w_row = jnp.exp(g_row - m_new)
        decay = jnp.exp(carry - m_new)
        aug = jnp.concatenate([vt.astype(F32) * w_row,
                               jnp.where(row16 == 0, jnp.broadcast_to(w_row, (16, L)), 0.0)], axis=0)
        upd = jnp.dot(aug.astype(BF16), kpair, preferred_element_type=F32)
        c_s[r] = decay * state + jnp.where(own_lanes, upd, 0.0)
        m_s[r] = jnp.broadcast_to(m_new, (1, L))


def _mlstm(qmt, km, vmt, gt, t):
    bsz, _, n = qmt.shape
    L = ML_CHUNK_LEN
    nch = n // L
    nctx_ch = (n - t) // L
    nlat = t // L

    def fwd_c(j):
        return j

    def bwd_c(j):
        return jnp.where(j < nctx_ch, nctx_ch - 1 - j, nch - 1 - (j - nctx_ch))

    def specs(cfn):
        return [
            pl.BlockSpec((1, 256, L), lambda b, j: (b, 0, cfn(j))),
            pl.BlockSpec((1, L, 256), lambda b, j: (b, cfn(j), 0)),
            pl.BlockSpec((1, 512, L), lambda b, j: (b, 0, cfn(j))),
            pl.BlockSpec((1, 16, L), lambda b, j: (b, 0, cfn(j))),
        ]

    lat = lambda j: jnp.maximum(j - nctx_ch, 0)
    out_specs = [
        pl.BlockSpec((1, L, 512), lambda b, j: (b, lat(j), 0)),
        pl.BlockSpec((1, L, 512), lambda b, j: (b, nlat - 1 - lat(j), 0)),
    ]
    out_shape = [jax.ShapeDtypeStruct((bsz, t, 512), F32)] * 2
    return pl.pallas_call(
        functools.partial(_mlstm_kernel, L=L),
        grid=(bsz, nch),
        in_specs=specs(fwd_c) + specs(bwd_c),
        out_specs=out_specs,
        out_shape=out_shape,
        scratch_shapes=[pltpu.VMEM((2 * ML_HEADS, ML_STATE_ROWS, 128), F32),
                        pltpu.VMEM((2 * ML_HEADS, 1, L), F32)],
        compiler_params=pltpu.CompilerParams(
            dimension_semantics=("arbitrary", "arbitrary"), vmem_limit_bytes=VMEM_LIMIT_BYTES),
        name="mlstm",
    )(qmt, km, vmt, gt, qmt, km, vmt, gt)


def _final_kernel(x_ref, da_ref, hf_ref, hb_ref, mo_ref, g1_ref, sh2_ref, sc2_ref, g2_ref,
                  mlg_ref, n2g_ref, fg_ref, wo_ref, w1_ref, w2_ref, o_ref):
    x = x_ref[0]
    hsum = hf_ref[0] + hb_ref[0]
    parts = []
    for h in range(ML_HEADS):
        u = hsum[:, h * ML_V:(h + 1) * ML_V]
        parts.append(u * lax.rsqrt(jnp.mean(u * u, axis=1, keepdims=True) + EPS))
    ml = jnp.concatenate(parts, axis=1) * mlg_ref[...] * mo_ref[0].astype(F32)
    cat = jnp.concatenate([da_ref[0], ml.astype(BF16)], axis=1)
    y = jnp.dot(cat, wo_ref[...], preferred_element_type=F32)
    x1 = x + g1_ref[0] * y
    xn = (_rms(x1, n2g_ref[...]) * (1.0 + sc2_ref[0]) + sh2_ref[0]).astype(BF16)
    hid = jnp.dot(xn, w1_ref[...], preferred_element_type=F32)
    hid = jnp.square(jnp.maximum(hid, 0.0)).astype(BF16)
    x2 = x1 + g2_ref[0] * jnp.dot(hid, w2_ref[...], preferred_element_type=F32)
    o_ref[0] = _rms(x2, fg_ref[...])


def _final(x, da, hf, hb, mo, g1, sh2, sc2, g2, mlg, n2g, fg, wo, w1, w2):
    bsz, t, d = x.shape
    tm = TOK_TILE
    moff = (mo.shape[1] - t) // tm
    tok = lambda w: pl.BlockSpec((1, tm, w), lambda b, i: (b, i, 0))
    mod = pl.BlockSpec((1, 1, d), lambda b, i: (b, 0, 0))
    const = lambda a: pl.BlockSpec(a.shape, lambda b, i: (0, 0), pipeline_mode=pl.Buffered(1))
    return pl.pallas_call(
        _final_kernel,
        grid=(bsz, t // tm),
        in_specs=[tok(d), tok(512), tok(512), tok(512),
                  pl.BlockSpec((1, tm, 512), lambda b, i: (b, i + moff, 0)),
                  mod, mod, mod, mod, const(mlg), const(n2g), const(fg), const(wo), const(w1), const(w2)],
        out_specs=tok(d),
        out_shape=jax.ShapeDtypeStruct((bsz, t, d), F32),
        compiler_params=pltpu.CompilerParams(
            dimension_semantics=("arbitrary", "arbitrary"), vmem_limit_bytes=VMEM_LIMIT_BYTES),
        name="final",
    )(x, da, hf, hb, mo, g1, sh2, sc2, g2, mlg, n2g, fg, wo, w1, w2)


def _rope_tables(t, nctx):
    rows = t // GRID_W
    row = jnp.repeat(jnp.arange(rows, dtype=F32), GRID_W)
    col = jnp.tile(jnp.arange(GRID_W, dtype=F32), rows)
    half = DA_QK // 2
    inv = ROPE_BASE ** (-jnp.arange(0, half, 2, dtype=F32) / half)
    ar = row[:, None] * inv
    ac = col[:, None] * inv
    ang = jnp.concatenate([ar, ar, ac, ac], axis=-1)
    cos = jnp.concatenate([jnp.ones((nctx, DA_QK), F32), jnp.cos(ang)], axis=0)
    sin = jnp.concatenate([jnp.zeros((nctx, DA_QK), F32), jnp.sin(ang)], axis=0)
    cos = jnp.tile(cos, (1, 2))
    sin = jnp.tile(sin, (1, 2))
    return cos, sin, cos.T, sin.T


def _rot_cols(w):
    k, n = w.shape
    w4 = w.reshape(k, n // 64, 2, 2, 16)
    rot = jnp.stack([-w4[:, :, :, 1], w4[:, :, :, 0]], axis=3)
    return rot.reshape(k, n)


def _split_weights(w_in_l, b_gate_l):
    nq = DA_HEADS * 2 * DA_QK
    o = 0
    wdq = w_in_l[:, o:o + nq]; o += nq
    wdk = w_in_l[:, o:o + nq]; o += nq
    wdv = w_in_l[:, o:o + DA_HEADS * DA_V]; o += DA_HEADS * DA_V
    wmq = w_in_l[:, o:o + ML_HEADS * ML_QK]; o += ML_HEADS * ML_QK
    wmk = w_in_l[:, o:o + ML_HEADS * ML_QK]; o += ML_HEADS * ML_QK
    wmv = w_in_l[:, o:o + ML_HEADS * ML_V]; o += ML_HEADS * ML_V
    wmo = w_in_l[:, o:o + ML_HEADS * ML_V]; o += ML_HEADS * ML_V
    wmg = w_in_l[:, o:o + 4 * ML_HEADS]
    perm = jnp.array([0, 1, 2, 3, 8, 9, 10, 11, 4, 5, 6, 7, 12, 13, 14, 15])
    wmg = wmg[:, perm]
    bg = b_gate_l[perm].reshape(16, 1).astype(F32)
    qscale = DA_QK ** -0.5 * LOG2_E
    kscale = ML_QK ** -0.5
    wn = jnp.concatenate([wdk, _rot_cols(wdk), wmk * kscale, wmo], axis=1).astype(BF16)
    wt = jnp.concatenate([wdq * qscale, _rot_cols(wdq) * qscale, wdv, wmq, wmv, wmg], axis=1).T.astype(BF16)
    return wn, wt, bg


def kernel(x, c, ctx, c_ctx, w_ada, b_ada, norm1_g, norm2_g, w_in, b_gate, lam_q1, lam_k1, lam_q2, lam_k2,
           subln_g, mlstm_norm_g, w_out, w_fc1, w_fc2, final_g):
    bsz, t, d = x.shape
    nctx = ctx.shape[1]
    depth = w_ada.shape[0]
    assert depth == 1, "single-layer block: the context stream is never updated"
    assert nctx % TOK_TILE == 0 and nctx % ML_CHUNK_LEN == 0 and ATT_TQ % TOK_TILE == 0 and ATT_TK % TOK_TILE == 0
    assert t % ATT_TQ == 0 and t % ML_CHUNK_LEN == 0 and (nctx + t) % ATT_TK == 0 and t % GRID_W == 0
    assert (t // ATT_TQ) * ((nctx + t) // ATT_TK) >= 3, "the attention pipeline needs at least three steps"
    lambda_init = 0.2

    cc = jnp.concatenate([c, c_ctx[None, :], jnp.zeros((8 - bsz - 1, d), F32)], axis=0)
    mod = _adaln(cc, w_ada[0], b_ada[0][None, :])
    mb = mod[:bsz].reshape(bsz, 1, 6, d)
    sh1, sc1, g1, sh2, sc2, g2 = [mb[:, :, k] for k in range(6)]
    mc = mod[bsz:bsz + 1].reshape(1, 1, 6, d)
    csh1, csc1 = mc[:, :, 0], mc[:, :, 1]

    wn, wt, bg = _split_weights(w_in[0], b_gate[0])
    cosn, sinn, cost, sint = _rope_tables(t, nctx)
    ka, km, mo, qat, vat, qmt, vmt, gt = _project(
        x, ctx, sc1, sh1, csc1, csh1, norm1_g[0][None, :], wn, wt, bg, cosn, sinn, cost, sint)

    lamv = jnp.stack([lam_q1[0], lam_k1[0], lam_q2[0], lam_k2[0]]).astype(F32)
    da = _attention(qat, ka, vat, lamv, subln_g[0].reshape(DA_V, 1), t, lambda_init)
    hf, hb = _mlstm(qmt, km, vmt, gt, t)

    return _final(x, da, hf, hb, mo, g1, sh2, sc2, g2, mlstm_norm_g[0][None, :], norm2_g[0][None, :],
                  final_g[None, :], w_out[0].astype(BF16), w_fc1[0].astype(BF16), w_fc2[0].astype(BF16))
```

```python
import functools

import jax
import jax.numpy as jnp
from jax import lax
from jax.experimental import pallas as pl
from jax.experimental.pallas import tpu as pltpu

DA_HEADS = 4
DA_QK = 64
DA_V = 128
ML_HEADS = 4
ML_QK = 64
ML_V = 128
GRID_W = 64
ROPE_BASE = 10000.0
EPS = 1e-6
LOG2_E = 1.4426950408889634

TOK_TILE = 256
ATT_TQ = 512
ATT_TK = 768
ML_CHUNK_LEN = 256
ML_STATE_ROWS = 144
VMEM_LIMIT_BYTES = 56 * 1024 * 1024

F32 = jnp.float32
BF16 = jnp.bfloat16


def _rms(x, g):
    return x * lax.rsqrt(jnp.mean(x * x, axis=-1, keepdims=True) + EPS) * g


def _adaln_kernel(c_ref, w_ref, b_ref, o_ref):
    c = c_ref[...]
    s = (c * jax.nn.sigmoid(c)).astype(BF16)
    o_ref[...] = jnp.dot(s, w_ref[...].astype(BF16), preferred_element_type=F32) + b_ref[...]


def _adaln(cc, w, b):
    rows, d = cc.shape
    n = w.shape[1]
    bn = d
    return pl.pallas_call(
        _adaln_kernel,
        grid=(n // bn,),
        in_specs=[pl.BlockSpec((rows, d), lambda j: (0, 0)),
                  pl.BlockSpec((d, bn), lambda j: (0, j)),
                  pl.BlockSpec((1, bn), lambda j: (0, j))],
        out_specs=pl.BlockSpec((rows, bn), lambda j: (0, j)),
        out_shape=jax.ShapeDtypeStruct((rows, n), F32),
        name="adaln",
    )(cc, w, b)


def _proj_kernel(x_ref, ctx_ref, sc_ref, sh_ref, csc_ref, csh_ref, g_ref, wn_ref, wt_ref, bg_ref,
                 cosn_ref, sinn_ref, cost_ref, sint_ref,
                 ka_ref, km_ref, mo_ref, qat_ref, vat_ref, qmt_ref, vmt_ref, gt_ref, cc_ref, *, nct):
    i = pl.program_id(1)
    is_ctx = i < nct
    x = jnp.where(is_ctx, ctx_ref[0], x_ref[0])
    sc = jnp.where(is_ctx, csc_ref[0], sc_ref[0])
    sh = jnp.where(is_ctx, csh_ref[0], sh_ref[0])
    xm = _rms(x, g_ref[...]) * (1.0 + sc) + sh
    xb = xm.astype(BF16)
    nt = (((1,), (1,)), ((), ()))
    hd = DA_HEADS * 2 * DA_QK
    ngate = 2 * hd + 768

    g = lax.dot_general(wt_ref[ngate:ngate + 16, :], xb, nt, preferred_element_type=F32) + bg_ref[...]
    li8 = g[0:8]
    lf8 = jnp.minimum(g[8:16], 0.0) - jnp.log1p(jnp.exp(-jnp.abs(g[8:16])))
    tm = g.shape[1]
    row8 = lax.broadcasted_iota(jnp.int32, (8, tm), 0)
    lane8 = lax.broadcasted_iota(jnp.int32, (8, tm), 1)
    is_f = row8 < ML_HEADS

    def scan(x, op, fill):
        step = 1
        while step < tm:
            prev = jnp.where(lane8 >= step, pltpu.roll(x, step, axis=1), fill)
            nxt = jnp.where(lane8 < tm - step, pltpu.roll(x, tm - step, axis=1), fill)
            x = op(x, jnp.where(is_f, prev, nxt))
            step *= 2
        return x

    b8 = scan(lf8, jnp.add, 0.0)
    c8 = li8 - b8
    cm8 = scan(c8, jnp.maximum, -jnp.inf)
    gt_ref[0] = jnp.concatenate([b8, c8, cm8], axis=0) * LOG2_E
    cc_ref[0] = (jnp.concatenate([c8, jnp.zeros((120, tm), F32)], axis=0) * LOG2_E).T

    hn = jnp.dot(xb, wn_ref[...], preferred_element_type=F32)
    ht = lax.dot_general(wt_ref[0:ngate, :], xb, nt, preferred_element_type=F32)

    cosn, sinn = cosn_ref[...], sinn_ref[...]
    cost, sint = cost_ref[...], sint_ref[...]
    lane = lax.broadcasted_iota(jnp.int32, (tm, 128), 1)
    low_half = (lane % 32) < 16
    for h in range(DA_HEADS):
        lo, hi = h * 128, (h + 1) * 128
        k = hn[:, lo:hi]
        kswap = jnp.where(low_half, pltpu.roll(k, 128 - 16, axis=1), pltpu.roll(k, 16, axis=1))
        ka_ref[0, :, lo:hi] = (k * cosn + kswap * sinn).astype(BF16)
        q = ht[lo:hi]
        qswap = jnp.concatenate([q[r0 + 16:r0 + 32] if part == 0 else q[r0:r0 + 16]
                                 for r0 in range(0, 128, 32) for part in (0, 1)], axis=0)
        q = (q * cost + qswap * sint).astype(BF16)
        qrow = lax.broadcasted_iota(jnp.int32, q.shape, 0)
        zero = jnp.zeros_like(q)
        qat_ref[0, h, 0] = jnp.concatenate(
            [jnp.where(qrow < DA_QK, q, zero), jnp.where(qrow >= DA_QK, q, zero)], axis=1)
    km_ref[0] = hn[:, hd:hd + 256].astype(BF16)
    mo_ref[0] = jax.nn.sigmoid(hn[:, hd + 256:hd + 768]).astype(BF16)

    vat_ref[0, 0] = ht[hd:2 * hd].astype(BF16)
    qmt_ref[0] = ht[2 * hd:2 * hd + 256].astype(BF16)
    vmt_ref[0] = ht[2 * hd + 256:2 * hd + 768].astype(BF16)


def _project(x, ctx, sc, sh, csc, csh, g, wn, wt, bg, cosn, sinn, cost, sint):
    bsz, t, d = x.shape
    nctx = ctx.shape[1]
    n = nctx + t
    tm = TOK_TILE
    nct = nctx // tm
    r = ATT_TK // tm
    hd = DA_HEADS * 2 * DA_QK
    const2 = lambda b, i: (0, 0)
    const3 = lambda b, i: (0, 0, 0)
    in_specs = [
        pl.BlockSpec((1, tm, d), lambda b, i: (b, jnp.maximum(i - nct, 0), 0)),
        pl.BlockSpec((1, tm, d), lambda b, i: (b, jnp.minimum(i, nct - 1), 0)),
        pl.BlockSpec((1, 1, d), lambda b, i: (b, 0, 0)),
        pl.BlockSpec((1, 1, d), lambda b, i: (b, 0, 0)),
        pl.BlockSpec((1, 1, d), const3),
        pl.BlockSpec((1, 1, d), const3),
        pl.BlockSpec((1, d), const2),
        pl.BlockSpec(wn.shape, const2),
        pl.BlockSpec(wt.shape, const2),
        pl.BlockSpec(bg.shape, const2),
        pl.BlockSpec((tm, 128), lambda b, i: (i, 0)),
        pl.BlockSpec((tm, 128), lambda b, i: (i, 0)),
        pl.BlockSpec((128, tm), lambda b, i: (0, i)),
        pl.BlockSpec((128, tm), lambda b, i: (0, i)),
    ]
    out_specs = [
        pl.BlockSpec((1, tm, hd), lambda b, i: (b, i, 0)),
        pl.BlockSpec((1, tm, 256), lambda b, i: (b, i, 0)),
        pl.BlockSpec((1, tm, 512), lambda b, i: (b, i, 0)),
        pl.BlockSpec((1, DA_HEADS, 1, 128, 2 * tm), lambda b, i: (b, 0, i, 0, 0)),
        pl.BlockSpec((1, 1, 512, tm), lambda b, i: (b, i // r, 0, i % r)),
        pl.BlockSpec((1, 256, tm), lambda b, i: (b, 0, i)),
        pl.BlockSpec((1, 512, tm), lambda b, i: (b, 0, i)),
        pl.BlockSpec((1, 24, tm), lambda b, i: (b, 0, i)),
        pl.BlockSpec((1, tm, 128), lambda b, i: (b, i, 0)),
    ]
    out_shape = [
        jax.ShapeDtypeStruct((bsz, n, hd), BF16),
        jax.ShapeDtypeStruct((bsz, n, 256), BF16),
        jax.ShapeDtypeStruct((bsz, n, 512), BF16),
        jax.ShapeDtypeStruct((bsz, DA_HEADS, n // tm, 128, 2 * tm), BF16),
        jax.ShapeDtypeStruct((bsz, n // ATT_TK, 512, ATT_TK), BF16),
        jax.ShapeDtypeStruct((bsz, 256, n), BF16),
        jax.ShapeDtypeStruct((bsz, 512, n), BF16),
        jax.ShapeDtypeStruct((bsz, 24, n), F32),
        jax.ShapeDtypeStruct((bsz, n, 128), F32),
    ]
    return pl.pallas_call(
        functools.partial(_proj_kernel, nct=nct),
        grid=(bsz, n // tm),
        in_specs=in_specs,
        out_specs=out_specs,
        out_shape=out_shape,
        compiler_params=pltpu.CompilerParams(
            dimension_semantics=("arbitrary", "arbitrary"), vmem_limit_bytes=VMEM_LIMIT_BYTES),
        name="project",
    )(x, ctx, sc, sh, csc, csh, g, wn, wt, bg, cosn, sinn, cost, sint)


def _attn_kernel(q_ref, k_ref, v_ref, lam_ref, g_ref, o_ref, m_s, l_s, lfin_s, acc_s, s_s, mb_s, al_s, p_s, *,
                 tq, tk, nblk, ntile, qsub, qoff, lambda_init):
    nstep = ntile * nblk
    sw = q_ref.shape[-1]
    half = sw // 2

    m_s[...] = jnp.zeros(m_s.shape, F32)
    l_s[...] = jnp.zeros(l_s.shape, F32)
    lfin_s[...] = jnp.ones(lfin_s.shape, F32)
    acc_s[...] = jnp.zeros(acc_s.shape, F32)

    lv = lam_ref[...]
    lam = (jnp.exp(jnp.sum(lv[0:1] * lv[1:2], axis=1, keepdims=True))
           - jnp.exp(jnp.sum(lv[2:3] * lv[3:4], axis=1, keepdims=True)) + lambda_init)

    def values(u):
        kb = lax.rem(u, nblk)
        pv = jnp.dot(v_ref[0, kb], p_s[...], preferred_element_type=F32)
        acc_s[...] = al_s[...] * acc_s[...] + pv

    def scale(u):
        kb = lax.rem(u, nblk)
        first = kb == 0
        m_old = jnp.where(first, -jnp.inf, m_s[...])
        l_old = jnp.where(first, 0.0, l_s[...])
        m_new = jnp.maximum(m_old, mb_s[...])
        alpha = jnp.exp2(m_old - m_new)
        p = jnp.exp2(s_s[...] - m_new)
        l_new = alpha * l_old + jnp.sum(p, axis=0, keepdims=True)
        l_s[...] = l_new
        lfin_s[...] = jnp.where(kb == nblk - 1, l_new, lfin_s[...])
        p_s[...] = p.astype(BF16)
        al_s[...] = alpha
        m_s[...] = m_new

    def scores(u):
        tile = lax.div(u, nblk)
        kb = lax.rem(u, nblk)
        qbd = jnp.concatenate([q_ref[0, 0, qoff + tile * qsub + c] for c in range(qsub)], axis=1)
        k = k_ref[0, pl.ds(pl.multiple_of(kb * tk, tk), tk), :]
        s = jnp.dot(k, qbd, preferred_element_type=F32)
        s_s[...] = s
        mb_s[...] = jnp.max(s, axis=0, keepdims=True)

    def finalize(tile):
        inv_l = 1.0 / lfin_s[...]
        acc = acc_s[...]
        for c in range(qsub):
            a1, a2 = c * sw, c * sw + half
            o = acc[:, a1:a2] * inv_l[:, a1:a2] - lam * (acc[:, a2:a2 + half] * inv_l[:, a2:a2 + half])
            y = o * lax.rsqrt(jnp.mean(o * o, axis=0, keepdims=True) + EPS) * g_ref[...] * (1.0 - lambda_init)
            o_ref[0, pl.ds(pl.multiple_of(tile * tq + c * half, half), half), :] = y.T.astype(BF16)

    scores(0)
    scale(0)
    scores(1)

    def body(u, carry):
        values(u - 1)
        scale(u)
        scores(u + 1)

        @pl.when(lax.rem(u - 1, nblk) == nblk - 1)
        def _():
            finalize(lax.div(u - 1, nblk))
        return carry

    lax.fori_loop(1, nstep - 1, body, 0)
    values(nstep - 2)
    scale(nstep - 1)
    values(nstep - 1)
    finalize(ntile - 1)


def _attention(qat, ka, vat, lamv, subg, t, lambda_init):
    bsz, _, nsub, _, sw = qat.shape
    n = ka.shape[1]
    tq, tk = ATT_TQ, ATT_TK
    sub = sw // 2
    nblk = n // tk
    kern = functools.partial(_attn_kernel, tq=tq, tk=tk, nblk=nblk, ntile=t // tq, qsub=tq // sub,
                             qoff=(n - t) // sub, lambda_init=lambda_init)
    row = lambda: pltpu.VMEM((1, 2 * tq), F32)
    return pl.pallas_call(
        kern,
        grid=(bsz, DA_HEADS),
        in_specs=[
            pl.BlockSpec((1, 1, nsub, 128, sw), lambda b, h: (b, h, 0, 0, 0)),
            pl.BlockSpec((1, n, 128), lambda b, h: (b, 0, h)),
            pl.BlockSpec((1, nblk, 128, tk), lambda b, h: (b, 0, h, 0)),
            pl.BlockSpec(lamv.shape, lambda b, h: (0, 0)),
            pl.BlockSpec(subg.shape, lambda b, h: (0, 0)),
        ],
        out_specs=pl.BlockSpec((1, t, 128), lambda b, h: (b, 0, h)),
        out_shape=jax.ShapeDtypeStruct((bsz, t, DA_HEADS * DA_V), BF16),
        scratch_shapes=[row(), row(), row(), pltpu.VMEM((DA_V, 2 * tq), F32), pltpu.VMEM((tk, 2 * tq), F32),
                        row(), row(), pltpu.VMEM((tk, 2 * tq), BF16)],
        compiler_params=pltpu.CompilerParams(
            dimension_semantics=("arbitrary", "arbitrary"), vmem_limit_bytes=VMEM_LIMIT_BYTES),
        name="diff_attn",
    )(qat, ka, vat, lamv, subg)


def _mlstm_kernel(qf_ref, kf_ref, vf_ref, gf_ref, cf_ref, qb_ref, kb_ref, vb_ref, gb_ref, cb_ref,
                  hf_ref, hb_ref, c_s, m_s, *, L):
    j = pl.program_id(1)

    @pl.when(j == 0)
    def _():
        c_s[...] = jnp.zeros(c_s.shape, F32)
        m_s[...] = jnp.zeros(m_s.shape, F32)

    sidx = lax.broadcasted_iota(jnp.int32, (L, L), 0)
    tidx = lax.broadcasted_iota(jnp.int32, (L, L), 1)
    row128 = lax.broadcasted_iota(jnp.int32, (128, L), 0)
    lane128 = lax.broadcasted_iota(jnp.int32, (ML_STATE_ROWS, 128), 1)
    row16 = lax.broadcasted_iota(jnp.int32, (16, L), 0)

    for r in range(2 * ML_HEADS):
        rev = r >= ML_HEADS
        h = r % ML_HEADS
        q_ref, k_ref, v_ref, g_ref, cc_ref = ((qb_ref, kb_ref, vb_ref, gb_ref, cb_ref) if rev
                                              else (qf_ref, kf_ref, vf_ref, gf_ref, cf_ref))
        out_ref = hb_ref if rev else hf_ref
        valid = (sidx >= tidx) if rev else (sidx <= tidx)
        pair, odd = h // 2, h % 2
        qpair = q_ref[0, pair * 128:(pair + 1) * 128, :]
        kpair = k_ref[0, :, pair * 128:(pair + 1) * 128]
        vt = v_ref[0, h * ML_V:(h + 1) * ML_V, :]
        own_rows = (row128 >= ML_QK) if odd else (row128 < ML_QK)
        own_lanes = (lane128 >= ML_QK) if odd else (lane128 < ML_QK)
        qt = jnp.where(own_rows, qpair, jnp.zeros_like(qpair))

        b_row = g_ref[0, r:r + 1, :]
        c_row = g_ref[0, 8 + r:9 + r, :]
        cm_row = g_ref[0, 16 + r:17 + r, :]
        c_col = cc_ref[0, :, r:r + 1]
        m_prev = m_s[r]
        state = c_s[r]

        m_run = jnp.maximum(m_prev, cm_row)
        m_t = b_row + m_run
        st = jnp.dot(kpair, qt, preferred_element_type=F32)
        sp = jnp.where(valid, st * jnp.exp2(c_col - m_run), 0.0)
        den_intra = jnp.sum(sp, axis=0, keepdims=True)
        num_intra = jnp.dot(vt, sp.astype(BF16), preferred_element_type=F32)
        cq = jnp.dot(state.astype(BF16), qpair, preferred_element_type=F32)
        w_inter = jnp.exp2(m_prev - m_run)
        num = num_intra + w_inter * cq[0:ML_V]
        den = den_intra + w_inter * cq[ML_V:ML_V + 1]
        ht = num * (1.0 / jnp.maximum(jnp.abs(den), jnp.exp2(-m_t)))
        out_ref[0, :, h * ML_V:(h + 1) * ML_V] = ht.T

        b_tot = b_row[:, 0:1] if rev else b_row[:, L - 1:L]
        cm_end = cm_row[:, 0:1] if rev else cm_row[:, L - 1:L]
        m_end = jnp.maximum(m_prev[:, 0:1], cm_end)
        w_row = jnp.exp2(c_row - m_end)
        decay = jnp.exp2(m_prev[:, 0:1] - m_end)
        aug = jnp.concatenate([vt.astype(F32) * w_row,
                               jnp.where(row16 == 0, jnp.broadcast_to(w_row, (16, L)), 0.0)], axis=0)
        upd = jnp.dot(aug.astype(BF16), kpair, preferred_element_type=F32)
        c_s[r] = decay * state + jnp.where(own_lanes, upd, 0.0)
        m_s[r] = jnp.broadcast_to(b_tot + m_end, (1, L))


def _mlstm(qmt, km, vmt, gt, cc, t):
    bsz, _, n = qmt.shape
    L = ML_CHUNK_LEN
    nch = n // L
    nctx_ch = (n - t) // L
    nlat = t // L

    def fwd_c(j):
        return j

    def bwd_c(j):
        return jnp.where(j < nctx_ch, nctx_ch - 1 - j, nch - 1 - (j - nctx_ch))

    def specs(cfn):
        return [
            pl.BlockSpec((1, 256, L), lambda b, j: (b, 0, cfn(j))),
            pl.BlockSpec((1, L, 256), lambda b, j: (b, cfn(j), 0)),
            pl.BlockSpec((1, 512, L), lambda b, j: (b, 0, cfn(j))),
            pl.BlockSpec((1, 24, L), lambda b, j: (b, 0, cfn(j))),
            pl.BlockSpec((1, L, 128), lambda b, j: (b, cfn(j), 0)),
        ]

    lat = lambda j: jnp.maximum(j - nctx_ch, 0)
    out_specs = [
        pl.BlockSpec((1, L, 512), lambda b, j: (b, lat(j), 0)),
        pl.BlockSpec((1, L, 512), lambda b, j: (b, nlat - 1 - lat(j), 0)),
    ]
    out_shape = [jax.ShapeDtypeStruct((bsz, t, 512), F32)] * 2
    return pl.pallas_call(
        functools.partial(_mlstm_kernel, L=L),
        grid=(bsz, nch),
        in_specs=specs(fwd_c) + specs(bwd_c),
        out_specs=out_specs,
        out_shape=out_shape,
        scratch_shapes=[pltpu.VMEM((2 * ML_HEADS, ML_STATE_ROWS, 128), F32),
                        pltpu.VMEM((2 * ML_HEADS, 1, L), F32)],
        compiler_params=pltpu.CompilerParams(
            dimension_semantics=("arbitrary", "arbitrary"), vmem_limit_bytes=VMEM_LIMIT_BYTES),
        name="mlstm",
    )(qmt, km, vmt, gt, cc, qmt, km, vmt, gt, cc)


def _final_kernel(x_ref, da_ref, hf_ref, hb_ref, mo_ref, g1_ref, sh2_ref, sc2_ref, g2_ref,
                  mlg_ref, n2g_ref, fg_ref, wo_ref, w1_ref, w2_ref, o_ref):
    x = x_ref[0]
    hsum = hf_ref[0] + hb_ref[0]
    parts = []
    for h in range(ML_HEADS):
        u = hsum[:, h * ML_V:(h + 1) * ML_V]
        parts.append(u * lax.rsqrt(jnp.mean(u * u, axis=1, keepdims=True) + EPS))
    ml = jnp.concatenate(parts, axis=1) * mlg_ref[...] * mo_ref[0].astype(F32)
    cat = jnp.concatenate([da_ref[0], ml.astype(BF16)], axis=1)
    y = jnp.dot(cat, wo_ref[...], preferred_element_type=F32)
    x1 = x + g1_ref[0] * y
    xn = (_rms(x1, n2g_ref[...]) * (1.0 + sc2_ref[0]) + sh2_ref[0]).astype(BF16)
    hid = jnp.dot(xn, w1_ref[...], preferred_element_type=F32)
    hid = jnp.square(jnp.maximum(hid, 0.0)).astype(BF16)
    x2 = x1 + g2_ref[0] * jnp.dot(hid, w2_ref[...], preferred_element_type=F32)
    o_ref[0] = _rms(x2, fg_ref[...])


def _final(x, da, hf, hb, mo, g1, sh2, sc2, g2, mlg, n2g, fg, wo, w1, w2):
    bsz, t, d = x.shape
    tm = TOK_TILE
    moff = (mo.shape[1] - t) // tm
    tok = lambda w: pl.BlockSpec((1, tm, w), lambda b, i: (b, i, 0))
    mod = pl.BlockSpec((1, 1, d), lambda b, i: (b, 0, 0))
    const = lambda a: pl.BlockSpec(a.shape, lambda b, i: (0, 0), pipeline_mode=pl.Buffered(1))
    return pl.pallas_call(
        _final_kernel,
        grid=(bsz, t // tm),
        in_specs=[tok(d), tok(512), tok(512), tok(512),
                  pl.BlockSpec((1, tm, 512), lambda b, i: (b, i + moff, 0)),
                  mod, mod, mod, mod, const(mlg), const(n2g), const(fg), const(wo), const(w1), const(w2)],
        out_specs=tok(d),
        out_shape=jax.ShapeDtypeStruct((bsz, t, d), F32),
        compiler_params=pltpu.CompilerParams(
            dimension_semantics=("arbitrary", "arbitrary"), vmem_limit_bytes=VMEM_LIMIT_BYTES),
        name="final",
    )(x, da, hf, hb, mo, g1, sh2, sc2, g2, mlg, n2g, fg, wo, w1, w2)


def _rope_tables(t, nctx):
    rows = t // GRID_W
    row = jnp.repeat(jnp.arange(rows, dtype=F32), GRID_W)
    col = jnp.tile(jnp.arange(GRID_W, dtype=F32), rows)
    half = DA_QK // 2
    inv = ROPE_BASE ** (-jnp.arange(0, half, 2, dtype=F32) / half)
    ar = row[:, None] * inv
    ac = col[:, None] * inv
    ang = jnp.concatenate([ar, ar, ac, ac], axis=-1)
    sign = jnp.where(jnp.arange(DA_QK) % 32 < 16, -1.0, 1.0).astype(F32)
    cos = jnp.concatenate([jnp.ones((nctx, DA_QK), F32), jnp.cos(ang)], axis=0)
    sin = jnp.concatenate([jnp.zeros((nctx, DA_QK), F32), jnp.sin(ang) * sign], axis=0)
    cos = jnp.tile(cos, (1, 2))
    sin = jnp.tile(sin, (1, 2))
    return cos, sin, cos.T, sin.T


def _split_weights(w_in_l, b_gate_l):
    nq = DA_HEADS * 2 * DA_QK
    o = 0
    wdq = w_in_l[:, o:o + nq]; o += nq
    wdk = w_in_l[:, o:o + nq]; o += nq
    wdv = w_in_l[:, o:o + DA_HEADS * DA_V]; o += DA_HEADS * DA_V
    wmq = w_in_l[:, o:o + ML_HEADS * ML_QK]; o += ML_HEADS * ML_QK
    wmk = w_in_l[:, o:o + ML_HEADS * ML_QK]; o += ML_HEADS * ML_QK
    wmv = w_in_l[:, o:o + ML_HEADS * ML_V]; o += ML_HEADS * ML_V
    wmo = w_in_l[:, o:o + ML_HEADS * ML_V]; o += ML_HEADS * ML_V
    wmg = w_in_l[:, o:o + 4 * ML_HEADS]
    perm = jnp.array([0, 1, 2, 3, 8, 9, 10, 11, 4, 5, 6, 7, 12, 13, 14, 15])
    wmg = wmg[:, perm]
    bg = b_gate_l[perm].reshape(16, 1).astype(F32)
    qscale = DA_QK ** -0.5 * LOG2_E
    kscale = ML_QK ** -0.5
    wn = jnp.concatenate([wdk, wmk * kscale, wmo], axis=1).astype(BF16)
    wt = jnp.concatenate([wdq * qscale, wdv, wmq, wmv, wmg], axis=1).T.astype(BF16)
    return wn, wt, bg


def kernel(x, c, ctx, c_ctx, w_ada, b_ada, norm1_g, norm2_g, w_in, b_gate, lam_q1, lam_k1, lam_q2, lam_k2,
           subln_g, mlstm_norm_g, w_out, w_fc1, w_fc2, final_g):
    bsz, t, d = x.shape
    nctx = ctx.shape[1]
    depth = w_ada.shape[0]
    assert depth == 1, "single-layer block: the context stream is never updated"
    assert TOK_TILE == ML_CHUNK_LEN, "the projection computes per-chunk gate sums on its own tile"
    assert nctx % TOK_TILE == 0 and ATT_TQ % TOK_TILE == 0 and ATT_TK % TOK_TILE == 0
    assert t % ATT_TQ == 0 and t % ML_CHUNK_LEN == 0 and (nctx + t) % ATT_TK == 0 and t % GRID_W == 0
    assert (t // ATT_TQ) * ((nctx + t) // ATT_TK) >= 3, "the attention pipeline needs at least three steps"
    lambda_init = 0.2

    cc = jnp.concatenate([c, c_ctx[None, :], jnp.zeros((8 - bsz - 1, d), F32)], axis=0)
    mod = _adaln(cc, w_ada[0], b_ada[0][None, :])
    mb = mod[:bsz].reshape(bsz, 1, 6, d)
    sh1, sc1, g1, sh2, sc2, g2 = [mb[:, :, k] for k in range(6)]
    mc = mod[bsz:bsz + 1].reshape(1, 1, 6, d)
    csh1, csc1 = mc[:, :, 0], mc[:, :, 1]

    wn, wt, bg = _split_weights(w_in[0], b_gate[0])
    cosn, sinn, cost, sint = _rope_tables(t, nctx)
    ka, km, mo, qat, vat, qmt, vmt, gt, cc = _project(
        x, ctx, sc1, sh1, csc1, csh1, norm1_g[0][None, :], wn, wt, bg, cosn, sinn, cost, sint)

    lamv = jnp.stack([lam_q1[0], lam_k1[0], lam_q2[0], lam_k2[0]]).astype(F32)
    da = _attention(qat, ka, vat, lamv, subln_g[0].reshape(DA_V, 1), t, lambda_init)
    hf, hb = _mlstm(qmt, km, vmt, gt, cc, t)

    return _final(x, da, hf, hb, mo, g1, sh2, sc2, g2, mlstm_norm_g[0][None, :], norm2_g[0][None, :],
                  final_g[None, :], w_out[0].astype(BF16), w_fc1[0].astype(BF16), w_fc2[0].astype(BF16))
```

```python
import functools

import jax
import jax.numpy as jnp
from jax import lax
from jax.experimental import pallas as pl
from jax.experimental.pallas import tpu as pltpu

DA_HEADS = 4
DA_QK = 64
DA_V = 128
ML_HEADS = 4
ML_QK = 64
ML_V = 128
GRID_W = 64
ROPE_BASE = 10000.0
EPS = 1e-6
LOG2_E = 1.4426950408889634

TOK_TILE = 256
FINAL_TILE = 512
ATT_TQ = 512
ATT_TK = 768
ML_CHUNK_LEN = 256
ML_STATE_ROWS = 144
VMEM_LIMIT_BYTES = 56 * 1024 * 1024

F32 = jnp.float32
BF16 = jnp.bfloat16


def _rms(x, g):
    return x * lax.rsqrt(jnp.mean(x * x, axis=-1, keepdims=True) + EPS) * g


def _adaln_kernel(c_ref, w_ref, b_ref, o_ref):
    c = c_ref[...]
    s = (c * jax.nn.sigmoid(c)).astype(BF16)
    o_ref[...] = jnp.dot(s, w_ref[...].astype(BF16), preferred_element_type=F32) + b_ref[...]


def _adaln(cc, w, b):
    rows, d = cc.shape
    n = w.shape[1]
    bn = d
    return pl.pallas_call(
        _adaln_kernel,
        grid=(n // bn,),
        in_specs=[pl.BlockSpec((rows, d), lambda j: (0, 0)),
                  pl.BlockSpec((d, bn), lambda j: (0, j)),
                  pl.BlockSpec((1, bn), lambda j: (0, j))],
        out_specs=pl.BlockSpec((rows, bn), lambda j: (0, j)),
        out_shape=jax.ShapeDtypeStruct((rows, n), F32),
        name="adaln",
    )(cc, w, b)


def _proj_kernel(x_ref, ctx_ref, sc_ref, sh_ref, csc_ref, csh_ref, g_ref, wn_ref, wt_ref, bg_ref,
                 cosn_ref, sinn_ref, cost_ref, sint_ref,
                 ka_ref, km_ref, mo_ref, qat_ref, vat_ref, qmt_ref, vmt_ref, gt_ref, cc_ref, *, nct):
    i = pl.program_id(1)
    is_ctx = i < nct
    x = jnp.where(is_ctx, ctx_ref[0], x_ref[0])
    sc = jnp.where(is_ctx, csc_ref[0], sc_ref[0])
    sh = jnp.where(is_ctx, csh_ref[0], sh_ref[0])
    xm = _rms(x, g_ref[...]) * (1.0 + sc) + sh
    xb = xm.astype(BF16)
    nt = (((1,), (1,)), ((), ()))
    hd = DA_HEADS * 2 * DA_QK
    ngate = 2 * hd + 768

    g = lax.dot_general(wt_ref[ngate:ngate + 16, :], xb, nt, preferred_element_type=F32) + bg_ref[...]
    li8 = g[0:8]
    lf8 = jnp.minimum(g[8:16], 0.0) - jnp.log1p(jnp.exp(-jnp.abs(g[8:16])))
    tm = g.shape[1]
    row8 = lax.broadcasted_iota(jnp.int32, (8, tm), 0)
    lane8 = lax.broadcasted_iota(jnp.int32, (8, tm), 1)
    is_f = row8 < ML_HEADS

    def scan(x, op, fill):
        step = 1
        while step < tm:
            prev = jnp.where(lane8 >= step, pltpu.roll(x, step, axis=1), fill)
            nxt = jnp.where(lane8 < tm - step, pltpu.roll(x, tm - step, axis=1), fill)
            x = op(x, jnp.where(is_f, prev, nxt))
            step *= 2
        return x

    b8 = scan(lf8, jnp.add, 0.0)
    c8 = li8 - b8
    cm8 = scan(c8, jnp.maximum, -jnp.inf)
    gt_ref[0] = jnp.concatenate([b8, c8, cm8], axis=0) * LOG2_E
    cc_ref[0] = (jnp.concatenate([c8, jnp.zeros((120, tm), F32)], axis=0) * LOG2_E).T

    hn = jnp.dot(xb, wn_ref[...], preferred_element_type=F32)
    ht = lax.dot_general(wt_ref[0:ngate, :], xb, nt, preferred_element_type=F32)

    cosn, sinn = cosn_ref[...], sinn_ref[...]
    cost, sint = cost_ref[...], sint_ref[...]
    lane = lax.broadcasted_iota(jnp.int32, (tm, 128), 1)
    low_half = (lane % 32) < 16
    for h in range(DA_HEADS):
        lo, hi = h * 128, (h + 1) * 128
        k = hn[:, lo:hi]
        kswap = jnp.where(low_half, pltpu.roll(k, 128 - 16, axis=1), pltpu.roll(k, 16, axis=1))
        ka_ref[0, :, lo:hi] = (k * cosn + kswap * sinn).astype(BF16)
        q = ht[lo:hi]
        qswap = jnp.concatenate([q[r0 + 16:r0 + 32] if part == 0 else q[r0:r0 + 16]
                                 for r0 in range(0, 128, 32) for part in (0, 1)], axis=0)
        q = (q * cost + qswap * sint).astype(BF16)
        qrow = lax.broadcasted_iota(jnp.int32, q.shape, 0)
        zero = jnp.zeros_like(q)
        qat_ref[0, h, 0] = jnp.concatenate(
            [jnp.where(qrow < DA_QK, q, zero), jnp.where(qrow >= DA_QK, q, zero)], axis=1)
    km_ref[0] = hn[:, hd:hd + 256].astype(BF16)
    mo_ref[0] = jax.nn.sigmoid(hn[:, hd + 256:hd + 768]).astype(BF16)

    vat_ref[0, 0] = ht[hd:2 * hd].astype(BF16)
    qmt_ref[0] = ht[2 * hd:2 * hd + 256].astype(BF16)
    vmt_ref[0] = ht[2 * hd + 256:2 * hd + 768].astype(BF16)


def _project(x, ctx, sc, sh, csc, csh, g, wn, wt, bg, cosn, sinn, cost, sint):
    bsz, t, d = x.shape
    nctx = ctx.shape[1]
    n = nctx + t
    tm = TOK_TILE
    nct = nctx // tm
    r = ATT_TK // tm
    hd = DA_HEADS * 2 * DA_QK
    const2 = lambda b, i: (0, 0)
    const3 = lambda b, i: (0, 0, 0)
    in_specs = [
        pl.BlockSpec((1, tm, d), lambda b, i: (b, jnp.maximum(i - nct, 0), 0)),
        pl.BlockSpec((1, tm, d), lambda b, i: (b, jnp.minimum(i, nct - 1), 0)),
        pl.BlockSpec((1, 1, d), lambda b, i: (b, 0, 0)),
        pl.BlockSpec((1, 1, d), lambda b, i: (b, 0, 0)),
        pl.BlockSpec((1, 1, d), const3),
        pl.BlockSpec((1, 1, d), const3),
        pl.BlockSpec((1, d), const2),
        pl.BlockSpec(wn.shape, const2),
        pl.BlockSpec(wt.shape, const2),
        pl.BlockSpec(bg.shape, const2),
        pl.BlockSpec((tm, 128), lambda b, i: (i, 0)),
        pl.BlockSpec((tm, 128), lambda b, i: (i, 0)),
        pl.BlockSpec((128, tm), lambda b, i: (0, i)),
        pl.BlockSpec((128, tm), lambda b, i: (0, i)),
    ]
    out_specs = [
        pl.BlockSpec((1, tm, hd), lambda b, i: (b, i, 0)),
        pl.BlockSpec((1, tm, 256), lambda b, i: (b, i, 0)),
        pl.BlockSpec((1, tm, 512), lambda b, i: (b, jnp.maximum(i - nct, 0), 0)),
        pl.BlockSpec((1, DA_HEADS, 1, 128, 2 * tm), lambda b, i: (b, 0, i, 0, 0)),
        pl.BlockSpec((1, 1, 512, tm), lambda b, i: (b, i // r, 0, i % r)),
        pl.BlockSpec((1, 256, tm), lambda b, i: (b, 0, i)),
        pl.BlockSpec((1, 512, tm), lambda b, i: (b, 0, i)),
        pl.BlockSpec((1, 24, tm), lambda b, i: (b, 0, i)),
        pl.BlockSpec((1, tm, 128), lambda b, i: (b, i, 0)),
    ]
    out_shape = [
        jax.ShapeDtypeStruct((bsz, n, hd), BF16),
        jax.ShapeDtypeStruct((bsz, n, 256), BF16),
        jax.ShapeDtypeStruct((bsz, t, 512), BF16),
        jax.ShapeDtypeStruct((bsz, DA_HEADS, n // tm, 128, 2 * tm), BF16),
        jax.ShapeDtypeStruct((bsz, n // ATT_TK, 512, ATT_TK), BF16),
        jax.ShapeDtypeStruct((bsz, 256, n), BF16),
        jax.ShapeDtypeStruct((bsz, 512, n), BF16),
        jax.ShapeDtypeStruct((bsz, 24, n), F32),
        jax.ShapeDtypeStruct((bsz, n, 128), F32),
    ]
    return pl.pallas_call(
        functools.partial(_proj_kernel, nct=nct),
        grid=(bsz, n // tm),
        in_specs=in_specs,
        out_specs=out_specs,
        out_shape=out_shape,
        compiler_params=pltpu.CompilerParams(
            dimension_semantics=("arbitrary", "arbitrary"), vmem_limit_bytes=VMEM_LIMIT_BYTES),
        name="project",
    )(x, ctx, sc, sh, csc, csh, g, wn, wt, bg, cosn, sinn, cost, sint)


def _attn_kernel(q_ref, k_ref, v_ref, lam_ref, g_ref, o_ref, m_s, l_s, lfin_s, acc_s, s_s, mb_s, al_s, p_s, *,
                 tq, tk, nblk, ntile, qsub, qoff, lambda_init):
    nstep = ntile * nblk
    sw = q_ref.shape[-1]
    half = sw // 2

    m_s[...] = jnp.zeros(m_s.shape, F32)
    l_s[...] = jnp.zeros(l_s.shape, F32)
    lfin_s[...] = jnp.ones(lfin_s.shape, F32)
    acc_s[...] = jnp.zeros(acc_s.shape, F32)

    lv = lam_ref[...]
    lam = (jnp.exp(jnp.sum(lv[0:1] * lv[1:2], axis=1, keepdims=True))
           - jnp.exp(jnp.sum(lv[2:3] * lv[3:4], axis=1, keepdims=True)) + lambda_init)

    def values(u):
        kb = lax.rem(u, nblk)
        pv = jnp.dot(v_ref[0, kb], p_s[...], preferred_element_type=F32)
        acc_s[...] = al_s[...] * acc_s[...] + pv

    def scale(u):
        kb = lax.rem(u, nblk)
        first = kb == 0
        m_old = jnp.where(first, -jnp.inf, m_s[...])
        l_old = jnp.where(first, 0.0, l_s[...])
        m_new = jnp.maximum(m_old, mb_s[...])
        alpha = jnp.exp2(m_old - m_new)
        p = jnp.exp2(s_s[...] - m_new)
        l_new = alpha * l_old + jnp.sum(p, axis=0, keepdims=True)
        l_s[...] = l_new
        lfin_s[...] = jnp.where(kb == nblk - 1, l_new, lfin_s[...])
        p_s[...] = p.astype(BF16)
        al_s[...] = alpha
        m_s[...] = m_new

    def scores(u):
        tile = lax.div(u, nblk)
        kb = lax.rem(u, nblk)
        qbd = jnp.concatenate([q_ref[0, 0, qoff + tile * qsub + c] for c in range(qsub)], axis=1)
        k = k_ref[0, pl.ds(pl.multiple_of(kb * tk, tk), tk), :]
        s = jnp.dot(k, qbd, preferred_element_type=F32)
        s_s[...] = s
        mb_s[...] = jnp.max(s, axis=0, keepdims=True)

    def finalize(tile):
        inv_l = 1.0 / lfin_s[...]
        acc = acc_s[...]
        for c in range(qsub):
            a1, a2 = c * sw, c * sw + half
            o = acc[:, a1:a2] * inv_l[:, a1:a2] - lam * (acc[:, a2:a2 + half] * inv_l[:, a2:a2 + half])
            y = o * lax.rsqrt(jnp.mean(o * o, axis=0, keepdims=True) + EPS) * g_ref[...] * (1.0 - lambda_init)
            o_ref[0, pl.ds(pl.multiple_of(tile * tq + c * half, half), half), :] = y.T.astype(BF16)

    scores(0)
    scale(0)
    scores(1)

    def body(u, carry):
        values(u - 1)
        scale(u)
        scores(u + 1)

        @pl.when(lax.rem(u - 1, nblk) == nblk - 1)
        def _():
            finalize(lax.div(u - 1, nblk))
        return carry

    lax.fori_loop(1, nstep - 1, body, 0)
    values(nstep - 2)
    scale(nstep - 1)
    values(nstep - 1)
    finalize(ntile - 1)


def _attention(qat, ka, vat, lamv, subg, t, lambda_init):
    bsz, _, nsub, _, sw = qat.shape
    n = ka.shape[1]
    tq, tk = ATT_TQ, ATT_TK
    sub = sw // 2
    nblk = n // tk
    kern = functools.partial(_attn_kernel, tq=tq, tk=tk, nblk=nblk, ntile=t // tq, qsub=tq // sub,
                             qoff=(n - t) // sub, lambda_init=lambda_init)
    row = lambda: pltpu.VMEM((1, 2 * tq), F32)
    return pl.pallas_call(
        kern,
        grid=(bsz, DA_HEADS),
        in_specs=[
            pl.BlockSpec((1, 1, nsub, 128, sw), lambda b, h: (b, h, 0, 0, 0)),
            pl.BlockSpec((1, n, 128), lambda b, h: (b, 0, h)),
            pl.BlockSpec((1, nblk, 128, tk), lambda b, h: (b, 0, h, 0)),
            pl.BlockSpec(lamv.shape, lambda b, h: (0, 0)),
            pl.BlockSpec(subg.shape, lambda b, h: (0, 0)),
        ],
        out_specs=pl.BlockSpec((1, t, 128), lambda b, h: (b, 0, h)),
        out_shape=jax.ShapeDtypeStruct((bsz, t, DA_HEADS * DA_V), BF16),
        scratch_shapes=[row(), row(), row(), pltpu.VMEM((DA_V, 2 * tq), F32), pltpu.VMEM((tk, 2 * tq), F32),
                        row(), row(), pltpu.VMEM((tk, 2 * tq), BF16)],
        compiler_params=pltpu.CompilerParams(
            dimension_semantics=("arbitrary", "arbitrary"), vmem_limit_bytes=VMEM_LIMIT_BYTES),
        name="diff_attn",
    )(qat, ka, vat, lamv, subg)


def _mlstm_kernel(qf_ref, kf_ref, vf_ref, gf_ref, cf_ref, qb_ref, kb_ref, vb_ref, gb_ref, cb_ref,
                  hf_ref, hb_ref, c_s, m_s, *, L):
    j = pl.program_id(1)

    @pl.when(j == 0)
    def _():
        c_s[...] = jnp.zeros(c_s.shape, F32)
        m_s[...] = jnp.zeros(m_s.shape, F32)

    sidx = lax.broadcasted_iota(jnp.int32, (L, L), 0)
    tidx = lax.broadcasted_iota(jnp.int32, (L, L), 1)
    row128 = lax.broadcasted_iota(jnp.int32, (128, L), 0)
    lane128 = lax.broadcasted_iota(jnp.int32, (ML_STATE_ROWS, 128), 1)
    row16 = lax.broadcasted_iota(jnp.int32, (16, L), 0)

    chains = range(2 * ML_HEADS)
    rev = [r >= ML_HEADS for r in chains]
    head = [r % ML_HEADS for r in chains]
    refs = [(qb_ref, kb_ref, vb_ref, gb_ref, cb_ref) if rev[r] else (qf_ref, kf_ref, vf_ref, gf_ref, cf_ref)
            for r in chains]
    pairs = [slice((head[r] // 2) * 128, (head[r] // 2 + 1) * 128) for r in chains]
    odd = [head[r] % 2 == 1 for r in chains]

    qpair = [refs[r][0][0, pairs[r], :] for r in chains]
    kpair = [refs[r][1][0, :, pairs[r]] for r in chains]
    vt = [refs[r][2][0, head[r] * ML_V:(head[r] + 1) * ML_V, :] for r in chains]
    b_row = [refs[r][3][0, r:r + 1, :] for r in chains]
    c_row = [refs[r][3][0, 8 + r:9 + r, :] for r in chains]
    cm_row = [refs[r][3][0, 16 + r:17 + r, :] for r in chains]
    c_col = [refs[r][4][0, :, r:r + 1] for r in chains]
    m_prev = [m_s[r] for r in chains]
    state = [c_s[r] for r in chains]
    m_run = [jnp.maximum(m_prev[r], cm_row[r]) for r in chains]

    qt = [jnp.where((row128 >= ML_QK) if odd[r] else (row128 < ML_QK), qpair[r], jnp.zeros_like(qpair[r]))
          for r in chains]
    st = [jnp.dot(kpair[r], qt[r], preferred_element_type=F32) for r in chains]
    cq = [jnp.dot(state[r].astype(BF16), qpair[r], preferred_element_type=F32) for r in chains]
    sp = [jnp.where((sidx >= tidx) if rev[r] else (sidx <= tidx),
                    st[r] * jnp.exp2(c_col[r] - m_run[r]), 0.0) for r in chains]
    den_intra = [jnp.sum(sp[r], axis=0, keepdims=True) for r in chains]
    num_intra = [jnp.dot(vt[r], sp[r].astype(BF16), preferred_element_type=F32) for r in chains]

    b_tot = [b_row[r][:, 0:1] if rev[r] else b_row[r][:, L - 1:L] for r in chains]
    cm_end = [cm_row[r][:, 0:1] if rev[r] else cm_row[r][:, L - 1:L] for r in chains]
    m_end = [jnp.maximum(m_prev[r][:, 0:1], cm_end[r]) for r in chains]
    w_row = [jnp.exp2(c_row[r] - m_end[r]) for r in chains]
    aug = [jnp.concatenate([vt[r].astype(F32) * w_row[r],
                            jnp.where(row16 == 0, jnp.broadcast_to(w_row[r], (16, L)), 0.0)], axis=0)
           for r in chains]
    upd = [jnp.dot(aug[r].astype(BF16), kpair[r], preferred_element_type=F32) for r in chains]

    for r in chains:
        w_inter = jnp.exp2(m_prev[r] - m_run[r])
        num = num_intra[r] + w_inter * cq[r][0:ML_V]
        den = den_intra[r] + w_inter * cq[r][ML_V:ML_V + 1]
        ht = num * (1.0 / jnp.maximum(jnp.abs(den), jnp.exp2(-(b_row[r] + m_run[r]))))
        out_ref = hb_ref if rev[r] else hf_ref
        out_ref[0, :, head[r] * ML_V:(head[r] + 1) * ML_V] = ht.T
        own_lanes = (lane128 >= ML_QK) if odd[r] else (lane128 < ML_QK)
        c_s[r] = jnp.exp2(m_prev[r][:, 0:1] - m_end[r]) * state[r] + jnp.where(own_lanes, upd[r], 0.0)
        m_s[r] = jnp.broadcast_to(b_tot[r] + m_end[r], (1, L))


def _mlstm(qmt, km, vmt, gt, cc, t):
    bsz, _, n = qmt.shape
    L = ML_CHUNK_LEN
    nch = n // L
    nctx_ch = (n - t) // L
    nlat = t // L

    def fwd_c(j):
        return j

    def bwd_c(j):
        return jnp.where(j < nctx_ch, nctx_ch - 1 - j, nch - 1 - (j - nctx_ch))

    def specs(cfn):
        return [
            pl.BlockSpec((1, 256, L), lambda b, j: (b, 0, cfn(j))),
            pl.BlockSpec((1, L, 256), lambda b, j: (b, cfn(j), 0)),
            pl.BlockSpec((1, 512, L), lambda b, j: (b, 0, cfn(j))),
            pl.BlockSpec((1, 24, L), lambda b, j: (b, 0, cfn(j))),
            pl.BlockSpec((1, L, 128), lambda b, j: (b, cfn(j), 0)),
        ]

    lat = lambda j: jnp.maximum(j - nctx_ch, 0)
    out_specs = [
        pl.BlockSpec((1, L, 512), lambda b, j: (b, lat(j), 0)),
        pl.BlockSpec((1, L, 512), lambda b, j: (b, nlat - 1 - lat(j), 0)),
    ]
    out_shape = [jax.ShapeDtypeStruct((bsz, t, 512), F32)] * 2
    return pl.pallas_call(
        functools.partial(_mlstm_kernel, L=L),
        grid=(bsz, nch),
        in_specs=specs(fwd_c) + specs(bwd_c),
        out_specs=out_specs,
        out_shape=out_shape,
        scratch_shapes=[pltpu.VMEM((2 * ML_HEADS, ML_STATE_ROWS, 128), F32),
                        pltpu.VMEM((2 * ML_HEADS, 1, L), F32)],
        compiler_params=pltpu.CompilerParams(
            dimension_semantics=("arbitrary", "arbitrary"), vmem_limit_bytes=VMEM_LIMIT_BYTES),
        name="mlstm",
    )(qmt, km, vmt, gt, cc, qmt, km, vmt, gt, cc)


def _final_kernel(x_ref, da_ref, hf_ref, hb_ref, mo_ref, g1_ref, sh2_ref, sc2_ref, g2_ref,
                  mlg_ref, n2g_ref, fg_ref, wo_ref, w1_ref, w2_ref, o_ref):
    x = x_ref[0]
    hsum = hf_ref[0] + hb_ref[0]
    parts = []
    for h in range(ML_HEADS):
        u = hsum[:, h * ML_V:(h + 1) * ML_V]
        parts.append(u * lax.rsqrt(jnp.mean(u * u, axis=1, keepdims=True) + EPS))
    ml = jnp.concatenate(parts, axis=1) * mlg_ref[...] * mo_ref[0].astype(F32)
    cat = jnp.concatenate([da_ref[0], ml.astype(BF16)], axis=1)
    y = jnp.dot(cat, wo_ref[...], preferred_element_type=F32)
    x1 = x + g1_ref[0] * y
    xn = (_rms(x1, n2g_ref[...]) * (1.0 + sc2_ref[0]) + sh2_ref[0]).astype(BF16)
    hid = jnp.dot(xn, w1_ref[...], preferred_element_type=F32)
    hid = jnp.square(jnp.maximum(hid, 0.0)).astype(BF16)
    x2 = x1 + g2_ref[0] * jnp.dot(hid, w2_ref[...], preferred_element_type=F32)
    o_ref[0] = _rms(x2, fg_ref[...])


def _final(x, da, hf, hb, mo, g1, sh2, sc2, g2, mlg, n2g, fg, wo, w1, w2):
    bsz, t, d = x.shape
    tm = FINAL_TILE
    tok = lambda w: pl.BlockSpec((1, tm, w), lambda b, i: (b, i, 0))
    mod = pl.BlockSpec((1, 1, d), lambda b, i: (b, 0, 0))
    const = lambda a: pl.BlockSpec(a.shape, lambda b, i: (0, 0), pipeline_mode=pl.Buffered(1))
    return pl.pallas_call(
        _final_kernel,
        grid=(bsz, t // tm),
        in_specs=[tok(d), tok(512), tok(512), tok(512), tok(512),
                  mod, mod, mod, mod, const(mlg), const(n2g), const(fg), const(wo), const(w1), const(w2)],
        out_specs=tok(d),
        out_shape=jax.ShapeDtypeStruct((bsz, t, d), F32),
        compiler_params=pltpu.CompilerParams(
            dimension_semantics=("arbitrary", "arbitrary"), vmem_limit_bytes=VMEM_LIMIT_BYTES),
        name="final",
    )(x, da, hf, hb, mo, g1, sh2, sc2, g2, mlg, n2g, fg, wo, w1, w2)


def _rope_tables(t, nctx):
    rows = t // GRID_W
    row = jnp.repeat(jnp.arange(rows, dtype=F32), GRID_W)
    col = jnp.tile(jnp.arange(GRID_W, dtype=F32), rows)
    half = DA_QK // 2
    inv = ROPE_BASE ** (-jnp.arange(0, half, 2, dtype=F32) / half)
    ar = row[:, None] * inv
    ac = col[:, None] * inv
    ang = jnp.concatenate([ar, ar, ac, ac], axis=-1)
    sign = jnp.where(jnp.arange(DA_QK) % 32 < 16, -1.0, 1.0).astype(F32)
    cos = jnp.concatenate([jnp.ones((nctx, DA_QK), F32), jnp.cos(ang)], axis=0)
    sin = jnp.concatenate([jnp.zeros((nctx, DA_QK), F32), jnp.sin(ang) * sign], axis=0)
    cos = jnp.tile(cos, (1, 2))
    sin = jnp.tile(sin, (1, 2))
    return cos, sin, cos.T, sin.T


def _split_weights(w_in_l, b_gate_l):
    nq = DA_HEADS * 2 * DA_QK
    o = 0
    wdq = w_in_l[:, o:o + nq]; o += nq
    wdk = w_in_l[:, o:o + nq]; o += nq
    wdv = w_in_l[:, o:o + DA_HEADS * DA_V]; o += DA_HEADS * DA_V
    wmq = w_in_l[:, o:o + ML_HEADS * ML_QK]; o += ML_HEADS * ML_QK
    wmk = w_in_l[:, o:o + ML_HEADS * ML_QK]; o += ML_HEADS * ML_QK
    wmv = w_in_l[:, o:o + ML_HEADS * ML_V]; o += ML_HEADS * ML_V
    wmo = w_in_l[:, o:o + ML_HEADS * ML_V]; o += ML_HEADS * ML_V
    wmg = w_in_l[:, o:o + 4 * ML_HEADS]
    perm = jnp.array([0, 1, 2, 3, 8, 9, 10, 11, 4, 5, 6, 7, 12, 13, 14, 15])
    wmg = wmg[:, perm]
    bg = b_gate_l[perm].reshape(16, 1).astype(F32)
    qscale = DA_QK ** -0.5 * LOG2_E
    kscale = ML_QK ** -0.5
    wn = jnp.concatenate([wdk, wmk * kscale, wmo], axis=1).astype(BF16)
    wt = jnp.concatenate([wdq * qscale, wdv, wmq, wmv, wmg], axis=1).T.astype(BF16)
    return wn, wt, bg


def kernel(x, c, ctx, c_ctx, w_ada, b_ada, norm1_g, norm2_g, w_in, b_gate, lam_q1, lam_k1, lam_q2, lam_k2,
           subln_g, mlstm_norm_g, w_out, w_fc1, w_fc2, final_g):
    bsz, t, d = x.shape
    nctx = ctx.shape[1]
    depth = w_ada.shape[0]
    assert depth == 1, "single-layer block: the context stream is never updated"
    assert TOK_TILE == ML_CHUNK_LEN, "the projection computes per-chunk gate sums on its own tile"
    assert nctx % TOK_TILE == 0 and ATT_TQ % TOK_TILE == 0 and ATT_TK % TOK_TILE == 0
    assert t % ATT_TQ == 0 and t % ML_CHUNK_LEN == 0 and (nctx + t) % ATT_TK == 0 and t % GRID_W == 0
    assert t % FINAL_TILE == 0
    assert (t // ATT_TQ) * ((nctx + t) // ATT_TK) >= 3, "the attention pipeline needs at least three steps"
    lambda_init = 0.2

    cc = jnp.concatenate([c, c_ctx[None, :], jnp.zeros((8 - bsz - 1, d), F32)], axis=0)
    mod = _adaln(cc, w_ada[0], b_ada[0][None, :])
    mb = mod[:bsz].reshape(bsz, 1, 6, d)
    sh1, sc1, g1, sh2, sc2, g2 = [mb[:, :, k] for k in range(6)]
    mc = mod[bsz:bsz + 1].reshape(1, 1, 6, d)
    csh1, csc1 = mc[:, :, 0], mc[:, :, 1]

    wn, wt, bg = _split_weights(w_in[0], b_gate[0])
    cosn, sinn, cost, sint = _rope_tables(t, nctx)
    ka, km, mo, qat, vat, qmt, vmt, gt, cc = _project(
        x, ctx, sc1, sh1, csc1, csh1, norm1_g[0][None, :], wn, wt, bg, cosn, sinn, cost, sint)

    lamv = jnp.stack([lam_q1[0], lam_k1[0], lam_q2[0], lam_k2[0]]).astype(F32)
    da = _attention(qat, ka, vat, lamv, subln_g[0].reshape(DA_V, 1), t, lambda_init)
    hf, hb = _mlstm(qmt, km, vmt, gt, cc, t)

    return _final(x, da, hf, hb, mo, g1, sh2, sc2, g2, mlstm_norm_g[0][None, :], norm2_g[0][None, :],
                  final_g[None, :], w_out[0].astype(BF16), w_fc1[0].astype(BF16), w_fc2[0].astype(BF16))
```

```python
import functools

import jax
import jax.numpy as jnp
from jax import lax
from jax.experimental import pallas as pl
from jax.experimental.pallas import tpu as pltpu

DA_HEADS = 4
DA_QK = 64
DA_V = 128
ML_HEADS = 4
ML_QK = 64
ML_V = 128
GRID_W = 64
ROPE_BASE = 10000.0
EPS = 1e-6
LOG2_E = 1.4426950408889634

TOK_TILE = 256
FINAL_TILE = 512
ATT_TQ = 512
ATT_TK = 768
ML_CHUNK_LEN = 256
ML_STATE_ROWS = 144
VMEM_LIMIT_BYTES = 56 * 1024 * 1024

F32 = jnp.float32
BF16 = jnp.bfloat16


def _rms(x, g):
    return x * lax.rsqrt(jnp.mean(x * x, axis=-1, keepdims=True) + EPS) * g


def _adaln_kernel(c_ref, w_ref, b_ref, o_ref):
    c = c_ref[...]
    s = (c * jax.nn.sigmoid(c)).astype(BF16)
    o_ref[...] = jnp.dot(s, w_ref[...].astype(BF16), preferred_element_type=F32) + b_ref[...]


def _adaln(cc, w, b):
    rows, d = cc.shape
    n = w.shape[1]
    bn = d
    return pl.pallas_call(
        _adaln_kernel,
        grid=(n // bn,),
        in_specs=[pl.BlockSpec((rows, d), lambda j: (0, 0)),
                  pl.BlockSpec((d, bn), lambda j: (0, j)),
                  pl.BlockSpec((1, bn), lambda j: (0, j))],
        out_specs=pl.BlockSpec((rows, bn), lambda j: (0, j)),
        out_shape=jax.ShapeDtypeStruct((rows, n), F32),
        name="adaln",
    )(cc, w, b)


def _proj_kernel(x_ref, ctx_ref, sc_ref, sh_ref, csc_ref, csh_ref, g_ref, wn_ref, wt_ref, bg_ref,
                 cosn_ref, sinn_ref, cost_ref, sint_ref,
                 ka_ref, km_ref, mo_ref, qat_ref, vat_ref, qmt_ref, vmt_ref, gt_ref, cc_ref, *, nct):
    i = pl.program_id(1)
    is_ctx = i < nct
    x = jnp.where(is_ctx, ctx_ref[0], x_ref[0])
    sc = jnp.where(is_ctx, csc_ref[0], sc_ref[0])
    sh = jnp.where(is_ctx, csh_ref[0], sh_ref[0])
    xm = _rms(x, g_ref[...]) * (1.0 + sc) + sh
    xb = xm.astype(BF16)
    nt = (((1,), (1,)), ((), ()))
    hd = DA_HEADS * 2 * DA_QK
    ngate = 2 * hd + 768

    g = lax.dot_general(wt_ref[ngate:ngate + 16, :], xb, nt, preferred_element_type=F32) + bg_ref[...]
    li8 = g[0:8]
    lf8 = jnp.minimum(g[8:16], 0.0) - jnp.log1p(jnp.exp(-jnp.abs(g[8:16])))
    tm = g.shape[1]
    row8 = lax.broadcasted_iota(jnp.int32, (8, tm), 0)
    lane8 = lax.broadcasted_iota(jnp.int32, (8, tm), 1)
    is_f = row8 < ML_HEADS

    def scan(x, op, fill):
        step = 1
        while step < tm:
            prev = jnp.where(lane8 >= step, pltpu.roll(x, step, axis=1), fill)
            nxt = jnp.where(lane8 < tm - step, pltpu.roll(x, tm - step, axis=1), fill)
            x = op(x, jnp.where(is_f, prev, nxt))
            step *= 2
        return x

    b8 = scan(lf8, jnp.add, 0.0)
    c8 = li8 - b8
    cm8 = scan(c8, jnp.maximum, -jnp.inf)
    gt_ref[0] = jnp.concatenate([b8, c8, cm8], axis=0) * LOG2_E
    cc_ref[0] = (jnp.concatenate([c8, jnp.zeros((120, tm), F32)], axis=0) * LOG2_E).T

    hn = jnp.dot(xb, wn_ref[...], preferred_element_type=F32)
    ht = lax.dot_general(wt_ref[0:ngate, :], xb, nt, preferred_element_type=F32)

    cosn, sinn = cosn_ref[...], sinn_ref[...]
    cost, sint = cost_ref[...], sint_ref[...]
    lane = lax.broadcasted_iota(jnp.int32, (tm, 128), 1)
    low_half = (lane % 32) < 16
    for h in range(DA_HEADS):
        lo, hi = h * 128, (h + 1) * 128
        k = hn[:, lo:hi]
        kswap = jnp.where(low_half, pltpu.roll(k, 128 - 16, axis=1), pltpu.roll(k, 16, axis=1))
        ka_ref[0, :, lo:hi] = (k * cosn + kswap * sinn).astype(BF16)
        q = ht[lo:hi]
        qswap = jnp.concatenate([q[r0 + 16:r0 + 32] if part == 0 else q[r0:r0 + 16]
                                 for r0 in range(0, 128, 32) for part in (0, 1)], axis=0)
        q = (q * cost + qswap * sint).astype(BF16)
        qrow = lax.broadcasted_iota(jnp.int32, q.shape, 0)
        zero = jnp.zeros_like(q)
        qat_ref[0, h, 0] = jnp.concatenate(
            [jnp.where(qrow < DA_QK, q, zero), jnp.where(qrow >= DA_QK, q, zero)], axis=1)
    km_ref[0] = hn[:, hd:hd + 256].astype(BF16)
    mo_ref[0] = jax.nn.sigmoid(hn[:, hd + 256:hd + 768]).astype(BF16)

    vat_ref[0, 0] = ht[hd:2 * hd].astype(BF16)
    qmt_ref[0] = ht[2 * hd:2 * hd + 256].astype(BF16)
    vmt_ref[0] = ht[2 * hd + 256:2 * hd + 768].astype(BF16)


def _project(x, ctx, sc, sh, csc, csh, g, wn, wt, bg, cosn, sinn, cost, sint):
    bsz, t, d = x.shape
    nctx = ctx.shape[1]
    n = nctx + t
    tm = TOK_TILE
    nct = nctx // tm
    r = ATT_TK // tm
    hd = DA_HEADS * 2 * DA_QK
    const2 = lambda b, i: (0, 0)
    const3 = lambda b, i: (0, 0, 0)
    in_specs = [
        pl.BlockSpec((1, tm, d), lambda b, i: (b, jnp.maximum(i - nct, 0), 0)),
        pl.BlockSpec((1, tm, d), lambda b, i: (b, jnp.minimum(i, nct - 1), 0)),
        pl.BlockSpec((1, 1, d), lambda b, i: (b, 0, 0)),
        pl.BlockSpec((1, 1, d), lambda b, i: (b, 0, 0)),
        pl.BlockSpec((1, 1, d), const3),
        pl.BlockSpec((1, 1, d), const3),
        pl.BlockSpec((1, d), const2),
        pl.BlockSpec(wn.shape, const2),
        pl.BlockSpec(wt.shape, const2),
        pl.BlockSpec(bg.shape, const2),
        pl.BlockSpec((tm, 128), lambda b, i: (i, 0)),
        pl.BlockSpec((tm, 128), lambda b, i: (i, 0)),
        pl.BlockSpec((128, tm), lambda b, i: (0, i)),
        pl.BlockSpec((128, tm), lambda b, i: (0, i)),
    ]
    out_specs = [
        pl.BlockSpec((1, tm, hd), lambda b, i: (b, i, 0)),
        pl.BlockSpec((1, tm, 256), lambda b, i: (b, i, 0)),
        pl.BlockSpec((1, tm, 512), lambda b, i: (b, jnp.maximum(i - nct, 0), 0)),
        pl.BlockSpec((1, DA_HEADS, 1, 128, 2 * tm), lambda b, i: (b, 0, i, 0, 0)),
        pl.BlockSpec((1, 1, 512, tm), lambda b, i: (b, i // r, 0, i % r)),
        pl.BlockSpec((1, 256, tm), lambda b, i: (b, 0, i)),
        pl.BlockSpec((1, 512, tm), lambda b, i: (b, 0, i)),
        pl.BlockSpec((1, 24, tm), lambda b, i: (b, 0, i)),
        pl.BlockSpec((1, tm, 128), lambda b, i: (b, i, 0)),
    ]
    out_shape = [
        jax.ShapeDtypeStruct((bsz, n, hd), BF16),
        jax.ShapeDtypeStruct((bsz, n, 256), BF16),
        jax.ShapeDtypeStruct((bsz, t, 512), BF16),
        jax.ShapeDtypeStruct((bsz, DA_HEADS, n // tm, 128, 2 * tm), BF16),
        jax.ShapeDtypeStruct((bsz, n // ATT_TK, 512, ATT_TK), BF16),
        jax.ShapeDtypeStruct((bsz, 256, n), BF16),
        jax.ShapeDtypeStruct((bsz, 512, n), BF16),
        jax.ShapeDtypeStruct((bsz, 24, n), F32),
        jax.ShapeDtypeStruct((bsz, n, 128), F32),
    ]
    return pl.pallas_call(
        functools.partial(_proj_kernel, nct=nct),
        grid=(bsz, n // tm),
        in_specs=in_specs,
        out_specs=out_specs,
        out_shape=out_shape,
        compiler_params=pltpu.CompilerParams(
            dimension_semantics=("arbitrary", "arbitrary"), vmem_limit_bytes=VMEM_LIMIT_BYTES),
        name="project",
    )(x, ctx, sc, sh, csc, csh, g, wn, wt, bg, cosn, sinn, cost, sint)


def _attn_kernel(q_ref, k_ref, v_ref, lam_ref, g_ref, o_ref, m_s, l_s, lfin_s, acc_s, s_s, mb_s, al_s, p_s, *,
                 tq, tk, nblk, ntile, qsub, qoff, lambda_init):
    nstep = ntile * nblk
    sw = q_ref.shape[-1]
    half = sw // 2

    m_s[...] = jnp.zeros(m_s.shape, F32)
    l_s[...] = jnp.zeros(l_s.shape, F32)
    lfin_s[...] = jnp.ones(lfin_s.shape, F32)
    acc_s[...] = jnp.zeros(acc_s.shape, F32)

    lv = lam_ref[...]
    lam = (jnp.exp(jnp.sum(lv[0:1] * lv[1:2], axis=1, keepdims=True))
           - jnp.exp(jnp.sum(lv[2:3] * lv[3:4], axis=1, keepdims=True)) + lambda_init)

    def values(u):
        kb = lax.rem(u, nblk)
        slot = lax.rem(lax.div(u, nblk), 2)
        pv = jnp.dot(v_ref[0, kb], p_s[...], preferred_element_type=F32)
        acc_s[slot] = al_s[...] * acc_s[slot] + pv

    def scale(u):
        kb = lax.rem(u, nblk)
        first = kb == 0
        m_old = jnp.where(first, -jnp.inf, m_s[...])
        l_old = jnp.where(first, 0.0, l_s[...])
        m_new = jnp.maximum(m_old, mb_s[...])
        alpha = jnp.exp2(m_old - m_new)
        p = jnp.exp2(s_s[...] - m_new)
        l_new = alpha * l_old + jnp.sum(p, axis=0, keepdims=True)
        l_s[...] = l_new
        lfin_s[...] = jnp.where(kb == nblk - 1, l_new, lfin_s[...])
        p_s[...] = p.astype(BF16)
        al_s[...] = alpha
        m_s[...] = m_new

    def scores(u):
        tile = lax.div(u, nblk)
        kb = lax.rem(u, nblk)
        qbd = jnp.concatenate([q_ref[0, 0, qoff + tile * qsub + c] for c in range(qsub)], axis=1)
        k = k_ref[0, pl.ds(pl.multiple_of(kb * tk, tk), tk), :]
        s = jnp.dot(k, qbd, preferred_element_type=F32)
        s_s[...] = s
        mb_s[...] = jnp.max(s, axis=0, keepdims=True)

    def finalize(tile):
        inv_l = 1.0 / lfin_s[...]
        acc = acc_s[lax.rem(tile, 2)]
        for c in range(qsub):
            a1, a2 = c * sw, c * sw + half
            o = acc[:, a1:a2] * inv_l[:, a1:a2] - lam * (acc[:, a2:a2 + half] * inv_l[:, a2:a2 + half])
            y = o * lax.rsqrt(jnp.mean(o * o, axis=0, keepdims=True) + EPS) * g_ref[...] * (1.0 - lambda_init)
            o_ref[0, pl.ds(pl.multiple_of(tile * tq + c * half, half), half), :] = y.T.astype(BF16)

    scores(0)
    scale(0)
    scores(1)

    def body(i, carry):
        u = 2 * i + 1
        values(u - 1)
        scale(u)
        scores(u + 1)
        values(u)
        scale(u + 1)
        scores(u + 2)
        for w in (u - 1, u):
            @pl.when(lax.rem(w, nblk) == nblk - 1)
            def _():
                finalize(lax.div(w, nblk))
        return carry

    assert nstep % 2 == 0 and nblk >= 2
    lax.fori_loop(0, (nstep - 2) // 2, body, 0)
    values(nstep - 2)
    scale(nstep - 1)
    values(nstep - 1)
    finalize(ntile - 1)


def _attention(qat, ka, vat, lamv, subg, t, lambda_init):
    bsz, _, nsub, _, sw = qat.shape
    n = ka.shape[1]
    tq, tk = ATT_TQ, ATT_TK
    sub = sw // 2
    nblk = n // tk
    kern = functools.partial(_attn_kernel, tq=tq, tk=tk, nblk=nblk, ntile=t // tq, qsub=tq // sub,
                             qoff=(n - t) // sub, lambda_init=lambda_init)
    row = lambda: pltpu.VMEM((1, 2 * tq), F32)
    return pl.pallas_call(
        kern,
        grid=(bsz, DA_HEADS),
        in_specs=[
            pl.BlockSpec((1, 1, nsub, 128, sw), lambda b, h: (b, h, 0, 0, 0)),
            pl.BlockSpec((1, n, 128), lambda b, h: (b, 0, h)),
            pl.BlockSpec((1, nblk, 128, tk), lambda b, h: (b, 0, h, 0)),
            pl.BlockSpec(lamv.shape, lambda b, h: (0, 0)),
            pl.BlockSpec(subg.shape, lambda b, h: (0, 0)),
        ],
        out_specs=pl.BlockSpec((1, t, 128), lambda b, h: (b, 0, h)),
        out_shape=jax.ShapeDtypeStruct((bsz, t, DA_HEADS * DA_V), BF16),
        scratch_shapes=[row(), row(), row(), pltpu.VMEM((2, DA_V, 2 * tq), F32), pltpu.VMEM((tk, 2 * tq), F32),
                        row(), row(), pltpu.VMEM((tk, 2 * tq), BF16)],
        compiler_params=pltpu.CompilerParams(
            dimension_semantics=("arbitrary", "arbitrary"), vmem_limit_bytes=VMEM_LIMIT_BYTES),
        name="diff_attn",
    )(qat, ka, vat, lamv, subg)


def _mlstm_kernel(qf_ref, kf_ref, vf_ref, gf_ref, cf_ref, qb_ref, kb_ref, vb_ref, gb_ref, cb_ref,
                  hf_ref, hb_ref, c_s, m_s, *, L):
    j = pl.program_id(1)

    @pl.when(j == 0)
    def _():
        c_s[...] = jnp.zeros(c_s.shape, F32)
        m_s[...] = jnp.zeros(m_s.shape, F32)

    sidx = lax.broadcasted_iota(jnp.int32, (L, L), 0)
    tidx = lax.broadcasted_iota(jnp.int32, (L, L), 1)
    row128 = lax.broadcasted_iota(jnp.int32, (128, L), 0)
    lane128 = lax.broadcasted_iota(jnp.int32, (ML_STATE_ROWS, 128), 1)
    row16 = lax.broadcasted_iota(jnp.int32, (16, L), 0)

    chains = range(2 * ML_HEADS)
    rev = [r >= ML_HEADS for r in chains]
    head = [r % ML_HEADS for r in chains]
    refs = [(qb_ref, kb_ref, vb_ref, gb_ref, cb_ref) if rev[r] else (qf_ref, kf_ref, vf_ref, gf_ref, cf_ref)
            for r in chains]
    pairs = [slice((head[r] // 2) * 128, (head[r] // 2 + 1) * 128) for r in chains]
    odd = [head[r] % 2 == 1 for r in chains]

    qpair = [refs[r][0][0, pairs[r], :] for r in chains]
    kpair = [refs[r][1][0, :, pairs[r]] for r in chains]
    vt = [refs[r][2][0, head[r] * ML_V:(head[r] + 1) * ML_V, :] for r in chains]
    b_row = [refs[r][3][0, r:r + 1, :] for r in chains]
    c_row = [refs[r][3][0, 8 + r:9 + r, :] for r in chains]
    cm_row = [refs[r][3][0, 16 + r:17 + r, :] for r in chains]
    c_col = [refs[r][4][0, :, r:r + 1] for r in chains]
    m_prev = [m_s[r] for r in chains]
    state = [c_s[r] for r in chains]
    m_run = [jnp.maximum(m_prev[r], cm_row[r]) for r in chains]

    qt = [jnp.where((row128 >= ML_QK) if odd[r] else (row128 < ML_QK), qpair[r], jnp.zeros_like(qpair[r]))
          for r in chains]
    st = [jnp.dot(kpair[r], qt[r], preferred_element_type=F32) for r in chains]
    cq = [jnp.dot(state[r].astype(BF16), qpair[r], preferred_element_type=F32) for r in chains]
    sp = [jnp.where((sidx >= tidx) if rev[r] else (sidx <= tidx),
                    st[r] * jnp.exp2(c_col[r] - m_run[r]), 0.0) for r in chains]
    den_intra = [jnp.sum(sp[r], axis=0, keepdims=True) for r in chains]
    num_intra = [jnp.dot(vt[r], sp[r].astype(BF16), preferred_element_type=F32) for r in chains]

    b_tot = [b_row[r][:, 0:1] if rev[r] else b_row[r][:, L - 1:L] for r in chains]
    cm_end = [cm_row[r][:, 0:1] if rev[r] else cm_row[r][:, L - 1:L] for r in chains]
    m_end = [jnp.maximum(m_prev[r][:, 0:1], cm_end[r]) for r in chains]
    w_row = [jnp.exp2(c_row[r] - m_end[r]) for r in chains]
    aug = [jnp.concatenate([vt[r].astype(F32) * w_row[r],
                            jnp.where(row16 == 0, jnp.broadcast_to(w_row[r], (16, L)), 0.0)], axis=0)
           for r in chains]
    upd = [jnp.dot(aug[r].astype(BF16), kpair[r], preferred_element_type=F32) for r in chains]

    for r in chains:
        w_inter = jnp.exp2(m_prev[r] - m_run[r])
        num = num_intra[r] + w_inter * cq[r][0:ML_V]
        den = den_intra[r] + w_inter * cq[r][ML_V:ML_V + 1]
        ht = num * (1.0 / jnp.maximum(jnp.abs(den), jnp.exp2(-(b_row[r] + m_run[r]))))
        out_ref = hb_ref if rev[r] else hf_ref
        out_ref[0, :, head[r] * ML_V:(head[r] + 1) * ML_V] = ht.T
        own_lanes = (lane128 >= ML_QK) if odd[r] else (lane128 < ML_QK)
        c_s[r] = jnp.exp2(m_prev[r][:, 0:1] - m_end[r]) * state[r] + jnp.where(own_lanes, upd[r], 0.0)
        m_s[r] = jnp.broadcast_to(b_tot[r] + m_end[r], (1, L))


def _mlstm(qmt, km, vmt, gt, cc, t):
    bsz, _, n = qmt.shape
    L = ML_CHUNK_LEN
    nch = n // L
    nctx_ch = (n - t) // L
    nlat = t // L

    def fwd_c(j):
        return j

    def bwd_c(j):
        return jnp.where(j < nctx_ch, nctx_ch - 1 - j, nch - 1 - (j - nctx_ch))

    def specs(cfn):
        return [
            pl.BlockSpec((1, 256, L), lambda b, j: (b, 0, cfn(j))),
            pl.BlockSpec((1, L, 256), lambda b, j: (b, cfn(j), 0)),
            pl.BlockSpec((1, 512, L), lambda b, j: (b, 0, cfn(j))),
            pl.BlockSpec((1, 24, L), lambda b, j: (b, 0, cfn(j))),
            pl.BlockSpec((1, L, 128), lambda b, j: (b, cfn(j), 0)),
        ]

    lat = lambda j: jnp.maximum(j - nctx_ch, 0)
    out_specs = [
        pl.BlockSpec((1, L, 512), lambda b, j: (b, lat(j), 0)),
        pl.BlockSpec((1, L, 512), lambda b, j: (b, nlat - 1 - lat(j), 0)),
    ]
    out_shape = [jax.ShapeDtypeStruct((bsz, t, 512), F32)] * 2
    return pl.pallas_call(
        functools.partial(_mlstm_kernel, L=L),
        grid=(bsz, nch),
        in_specs=specs(fwd_c) + specs(bwd_c),
        out_specs=out_specs,
        out_shape=out_shape,
        scratch_shapes=[pltpu.VMEM((2 * ML_HEADS, ML_STATE_ROWS, 128), F32),
                        pltpu.VMEM((2 * ML_HEADS, 1, L), F32)],
        compiler_params=pltpu.CompilerParams(
            dimension_semantics=("arbitrary", "arbitrary"), vmem_limit_bytes=VMEM_LIMIT_BYTES),
        name="mlstm",
    )(qmt, km, vmt, gt, cc, qmt, km, vmt, gt, cc)


def _final_kernel(x_ref, da_ref, hf_ref, hb_ref, mo_ref, g1_ref, sh2_ref, sc2_ref, g2_ref,
                  mlg_ref, n2g_ref, fg_ref, wo_ref, w1_ref, w2_ref, o_ref):
    x = x_ref[0]
    hsum = hf_ref[0] + hb_ref[0]
    parts = []
    for h in range(ML_HEADS):
        u = hsum[:, h * ML_V:(h + 1) * ML_V]
        parts.append(u * lax.rsqrt(jnp.mean(u * u, axis=1, keepdims=True) + EPS))
    ml = jnp.concatenate(parts, axis=1) * mlg_ref[...] * mo_ref[0].astype(F32)
    cat = jnp.concatenate([da_ref[0], ml.astype(BF16)], axis=1)
    y = jnp.dot(cat, wo_ref[...], preferred_element_type=F32)
    x1 = x + g1_ref[0] * y
    xn = (_rms(x1, n2g_ref[...]) * (1.0 + sc2_ref[0]) + sh2_ref[0]).astype(BF16)
    hid = jnp.dot(xn, w1_ref[...], preferred_element_type=F32)
    hid = jnp.square(jnp.maximum(hid, 0.0)).astype(BF16)
    x2 = x1 + g2_ref[0] * jnp.dot(hid, w2_ref[...], preferred_element_type=F32)
    o_ref[0] = _rms(x2, fg_ref[...])


def _final(x, da, hf, hb, mo, g1, sh2, sc2, g2, mlg, n2g, fg, wo, w1, w2):
    bsz, t, d = x.shape
    tm = FINAL_TILE
    tok = lambda w: pl.BlockSpec((1, tm, w), lambda b, i: (b, i, 0))
    mod = pl.BlockSpec((1, 1, d), lambda b, i: (b, 0, 0))
    const = lambda a: pl.BlockSpec(a.shape, lambda b, i: (0, 0), pipeline_mode=pl.Buffered(1))
    return pl.pallas_call(
        _final_kernel,
        grid=(bsz, t // tm),
        in_specs=[tok(d), tok(512), tok(512), tok(512), tok(512),
                  mod, mod, mod, mod, const(mlg), const(n2g), const(fg), const(wo), const(w1), const(w2)],
        out_specs=tok(d),
        out_shape=jax.ShapeDtypeStruct((bsz, t, d), F32),
        compiler_params=pltpu.CompilerParams(
            dimension_semantics=("arbitrary", "arbitrary"), vmem_limit_bytes=VMEM_LIMIT_BYTES),
        name="final",
    )(x, da, hf, hb, mo, g1, sh2, sc2, g2, mlg, n2g, fg, wo, w1, w2)


def _rope_tables(t, nctx):
    rows = t // GRID_W
    row = jnp.repeat(jnp.arange(rows, dtype=F32), GRID_W)
    col = jnp.tile(jnp.arange(GRID_W, dtype=F32), rows)
    half = DA_QK // 2
    inv = ROPE_BASE ** (-jnp.arange(0, half, 2, dtype=F32) / half)
    ar = row[:, None] * inv
    ac = col[:, None] * inv
    ang = jnp.concatenate([ar, ar, ac, ac], axis=-1)
    sign = jnp.where(jnp.arange(DA_QK) % 32 < 16, -1.0, 1.0).astype(F32)
    cos = jnp.concatenate([jnp.ones((nctx, DA_QK), F32), jnp.cos(ang)], axis=0)
    sin = jnp.concatenate([jnp.zeros((nctx, DA_QK), F32), jnp.sin(ang) * sign], axis=0)
    cos = jnp.tile(cos, (1, 2))
    sin = jnp.tile(sin, (1, 2))
    return cos, sin, cos.T, sin.T


def _split_weights(w_in_l, b_gate_l):
    nq = DA_HEADS * 2 * DA_QK
    o = 0
    wdq = w_in_l[:, o:o + nq]; o += nq
    wdk = w_in_l[:, o:o + nq]; o += nq
    wdv = w_in_l[:, o:o + DA_HEADS * DA_V]; o += DA_HEADS * DA_V
    wmq = w_in_l[:, o:o + ML_HEADS * ML_QK]; o += ML_HEADS * ML_QK
    wmk = w_in_l[:, o:o + ML_HEADS * ML_QK]; o += ML_HEADS * ML_QK
    wmv = w_in_l[:, o:o + ML_HEADS * ML_V]; o += ML_HEADS * ML_V
    wmo = w_in_l[:, o:o + ML_HEADS * ML_V]; o += ML_HEADS * ML_V
    wmg = w_in_l[:, o:o + 4 * ML_HEADS]
    perm = jnp.array([0, 1, 2, 3, 8, 9, 10, 11, 4, 5, 6, 7, 12, 13, 14, 15])
    wmg = wmg[:, perm]
    bg = b_gate_l[perm].reshape(16, 1).astype(F32)
    qscale = DA_QK ** -0.5 * LOG2_E
    kscale = ML_QK ** -0.5
    wn = jnp.concatenate([wdk, wmk * kscale, wmo], axis=1).astype(BF16)
    wt = jnp.concatenate([wdq * qscale, wdv, wmq, wmv, wmg], axis=1).T.astype(BF16)
    return wn, wt, bg


def kernel(x, c, ctx, c_ctx, w_ada, b_ada, norm1_g, norm2_g, w_in, b_gate, lam_q1, lam_k1, lam_q2, lam_k2,
           subln_g, mlstm_norm_g, w_out, w_fc1, w_fc2, final_g):
    bsz, t, d = x.shape
    nctx = ctx.shape[1]
    depth = w_ada.shape[0]
    assert depth == 1, "single-layer block: the context stream is never updated"
    assert TOK_TILE == ML_CHUNK_LEN, "the projection computes per-chunk gate sums on its own tile"
    assert nctx % TOK_TILE == 0 and ATT_TQ % TOK_TILE == 0 and ATT_TK % TOK_TILE == 0
    assert t % ATT_TQ == 0 and t % ML_CHUNK_LEN == 0 and (nctx + t) % ATT_TK == 0 and t % GRID_W == 0
    assert t % FINAL_TILE == 0
    assert (t // ATT_TQ) * ((nctx + t) // ATT_TK) >= 3, "the attention pipeline needs at least three steps"
    lambda_init = 0.2

    cc = jnp.concatenate([c, c_ctx[None, :], jnp.zeros((8 - bsz - 1, d), F32)], axis=0)
    mod = _adaln(cc, w_ada[0], b_ada[0][None, :])
    mb = mod[:bsz].reshape(bsz, 1, 6, d)
    sh1, sc1, g1, sh2, sc2, g2 = [mb[:, :, k] for k in range(6)]
    mc = mod[bsz:bsz + 1].reshape(1, 1, 6, d)
    csh1, csc1 = mc[:, :, 0], mc[:, :, 1]

    wn, wt, bg = _split_weights(w_in[0], b_gate[0])
    cosn, sinn, cost, sint = _rope_tables(t, nctx)
    ka, km, mo, qat, vat, qmt, vmt, gt, cc = _project(
        x, ctx, sc1, sh1, csc1, csh1, norm1_g[0][None, :], wn, wt, bg, cosn, sinn, cost, sint)

    lamv = jnp.stack([lam_q1[0], lam_k1[0], lam_q2[0], lam_k2[0]]).astype(F32)
    da = _attention(qat, ka, vat, lamv, subln_g[0].reshape(DA_V, 1), t, lambda_init)
    hf, hb = _mlstm(qmt, km, vmt, gt, cc, t)

    return _final(x, da, hf, hb, mo, g1, sh2, sc2, g2, mlstm_norm_g[0][None, :], norm2_g[0][None, :],
                  final_g[None, :], w_out[0].astype(BF16), w_fc1[0].astype(BF16), w_fc2[0].astype(BF16))
```

```python
import functools

import jax
import jax.numpy as jnp
from jax import lax
from jax.experimental import pallas as pl
from jax.experimental.pallas import tpu as pltpu

DA_HEADS = 4
DA_QK = 64
DA_V = 128
ML_HEADS = 4
ML_QK = 64
ML_V = 128
GRID_W = 64
ROPE_BASE = 10000.0
EPS = 1e-6
LOG2_E = 1.4426950408889634

TOK_TILE = 256
FINAL_TILE = 512
ATT_TQ = 512
ATT_TK = 768
ATT_MAX_UNROLL = 2
ML_CHUNK_LEN = 256
ML_STATE_ROWS = 144
VMEM_LIMIT_BYTES = 56 * 1024 * 1024

F32 = jnp.float32
BF16 = jnp.bfloat16


def _rms(x, g):
    return x * lax.rsqrt(jnp.mean(x * x, axis=-1, keepdims=True) + EPS) * g


def _adaln_kernel(c_ref, w_ref, b_ref, o_ref):
    c = c_ref[...]
    s = (c * jax.nn.sigmoid(c)).astype(BF16)
    o_ref[...] = jnp.dot(s, w_ref[...].astype(BF16), preferred_element_type=F32) + b_ref[...]


def _adaln(cc, w, b):
    rows, d = cc.shape
    n = w.shape[1]
    bn = d
    return pl.pallas_call(
        _adaln_kernel,
        grid=(n // bn,),
        in_specs=[pl.BlockSpec((rows, d), lambda j: (0, 0)),
                  pl.BlockSpec((d, bn), lambda j: (0, j)),
                  pl.BlockSpec((1, bn), lambda j: (0, j))],
        out_specs=pl.BlockSpec((rows, bn), lambda j: (0, j)),
        out_shape=jax.ShapeDtypeStruct((rows, n), F32),
        name="adaln",
    )(cc, w, b)


def _proj_kernel(x_ref, ctx_ref, sc_ref, sh_ref, csc_ref, csh_ref, g_ref, wn_ref, wt_ref, bg_ref,
                 cosn_ref, sinn_ref,
                 ka_ref, km_ref, mo_ref, qat_ref, vat_ref, qmt_ref, vmt_ref, gt_ref, cc_ref, *, nct):
    i = pl.program_id(0)
    is_ctx = i < nct
    x = jnp.where(is_ctx, ctx_ref[0], x_ref[0])
    sc = jnp.where(is_ctx, csc_ref[0], sc_ref[0])
    sh = jnp.where(is_ctx, csh_ref[0], sh_ref[0])
    xm = _rms(x, g_ref[...]) * (1.0 + sc) + sh
    xb = xm.astype(BF16)
    nt = (((1,), (1,)), ((), ()))
    hd = DA_HEADS * 2 * DA_QK
    ngate = 2 * hd + 768

    g = lax.dot_general(wt_ref[ngate:ngate + 16, :], xb, nt, preferred_element_type=F32) + bg_ref[...]
    li8 = g[0:8]
    lf8 = jnp.minimum(g[8:16], 0.0) - jnp.log1p(jnp.exp(-jnp.abs(g[8:16])))
    tm = g.shape[1]
    row8 = lax.broadcasted_iota(jnp.int32, (8, tm), 0)
    lane8 = lax.broadcasted_iota(jnp.int32, (8, tm), 1)
    is_f = row8 < ML_HEADS

    def scan(x, op, fill):
        step = 1
        while step < tm:
            prev = jnp.where(lane8 >= step, pltpu.roll(x, step, axis=1), fill)
            nxt = jnp.where(lane8 < tm - step, pltpu.roll(x, tm - step, axis=1), fill)
            x = op(x, jnp.where(is_f, prev, nxt))
            step *= 2
        return x

    b8 = scan(lf8, jnp.add, 0.0)
    c8 = li8 - b8
    cm8 = scan(c8, jnp.maximum, -jnp.inf)
    gt_ref[0] = jnp.concatenate([b8, c8, cm8], axis=0) * LOG2_E
    cc_ref[0] = (jnp.concatenate([c8, jnp.zeros((120, tm), F32)], axis=0) * LOG2_E).T

    hn = jnp.dot(xb, wn_ref[...], preferred_element_type=F32)
    ht = lax.dot_general(wt_ref[0:ngate, :], xb, nt, preferred_element_type=F32)

    cosn, sinn = cosn_ref[...], sinn_ref[...]
    cost, sint = cosn.T, sinn.T
    lane = lax.broadcasted_iota(jnp.int32, (tm, 128), 1)
    low_half = (lane % 32) < 16
    for h in range(DA_HEADS):
        lo, hi = h * 128, (h + 1) * 128
        k = hn[:, lo:hi]
        kswap = jnp.where(low_half, pltpu.roll(k, 128 - 16, axis=1), pltpu.roll(k, 16, axis=1))
        ka_ref[0, :, lo:hi] = (k * cosn + kswap * sinn).astype(BF16)
        q = ht[lo:hi]
        qswap = jnp.concatenate([q[r0 + 16:r0 + 32] if part == 0 else q[r0:r0 + 16]
                                 for r0 in range(0, 128, 32) for part in (0, 1)], axis=0)
        q = (q * cost + qswap * sint).astype(BF16)
        qrow = lax.broadcasted_iota(jnp.int32, q.shape, 0)
        zero = jnp.zeros_like(q)
        qat_ref[0, h, 0] = jnp.concatenate(
            [jnp.where(qrow < DA_QK, q, zero), jnp.where(qrow >= DA_QK, q, zero)], axis=1)
    km_ref[0] = hn[:, hd:hd + 256].astype(BF16)
    mo_ref[0] = jax.nn.sigmoid(hn[:, hd + 256:hd + 768]).astype(BF16)

    vat_ref[0, 0] = ht[hd:2 * hd].astype(BF16)
    qmt_ref[0] = ht[2 * hd:2 * hd + 256].astype(BF16)
    vmt_ref[0] = ht[2 * hd + 256:2 * hd + 768].astype(BF16)


def _project(x, ctx, sc, sh, csc, csh, g, wn, wt, bg, cosn, sinn):
    bsz, t, d = x.shape
    nctx = ctx.shape[1]
    n = nctx + t
    tm = TOK_TILE
    nct = nctx // tm
    r = ATT_TK // tm
    hd = DA_HEADS * 2 * DA_QK
    const2 = lambda i, b: (0, 0)
    const3 = lambda i, b: (0, 0, 0)
    in_specs = [
        pl.BlockSpec((1, tm, d), lambda i, b: (b, jnp.maximum(i - nct, 0), 0)),
        pl.BlockSpec((1, tm, d), lambda i, b: (b, jnp.minimum(i, nct - 1), 0)),
        pl.BlockSpec((1, 1, d), lambda i, b: (b, 0, 0)),
        pl.BlockSpec((1, 1, d), lambda i, b: (b, 0, 0)),
        pl.BlockSpec((1, 1, d), const3),
        pl.BlockSpec((1, 1, d), const3),
        pl.BlockSpec((1, d), const2),
        pl.BlockSpec(wn.shape, const2),
        pl.BlockSpec(wt.shape, const2),
        pl.BlockSpec(bg.shape, const2),
        pl.BlockSpec((tm, 128), lambda i, b: (i, 0)),
        pl.BlockSpec((tm, 128), lambda i, b: (i, 0)),
    ]
    out_specs = [
        pl.BlockSpec((1, tm, hd), lambda i, b: (b, i, 0)),
        pl.BlockSpec((1, tm, 256), lambda i, b: (b, i, 0)),
        pl.BlockSpec((1, tm, 512), lambda i, b: (b, jnp.where(i < nct, n // tm - nct + i, i - nct), 0)),
        pl.BlockSpec((1, DA_HEADS, 1, 128, 2 * tm), lambda i, b: (b, 0, i, 0, 0)),
        pl.BlockSpec((1, 1, 512, tm), lambda i, b: (b, i // r, 0, i % r)),
        pl.BlockSpec((1, 256, tm), lambda i, b: (b, 0, i)),
        pl.BlockSpec((1, 512, tm), lambda i, b: (b, 0, i)),
        pl.BlockSpec((1, 24, tm), lambda i, b: (b, 0, i)),
        pl.BlockSpec((1, tm, 128), lambda i, b: (b, i, 0)),
    ]
    out_shape = [
        jax.ShapeDtypeStruct((bsz, n, hd), BF16),
        jax.ShapeDtypeStruct((bsz, n, 256), BF16),
        jax.ShapeDtypeStruct((bsz, n, 512), BF16),
        jax.ShapeDtypeStruct((bsz, DA_HEADS, n // tm, 128, 2 * tm), BF16),
        jax.ShapeDtypeStruct((bsz, n // ATT_TK, 512, ATT_TK), BF16),
        jax.ShapeDtypeStruct((bsz, 256, n), BF16),
        jax.ShapeDtypeStruct((bsz, 512, n), BF16),
        jax.ShapeDtypeStruct((bsz, 24, n), F32),
        jax.ShapeDtypeStruct((bsz, n, 128), F32),
    ]
    return pl.pallas_call(
        functools.partial(_proj_kernel, nct=nct),
        grid=(n // tm, bsz),
        in_specs=in_specs,
        out_specs=out_specs,
        out_shape=out_shape,
        compiler_params=pltpu.CompilerParams(
            dimension_semantics=("arbitrary", "arbitrary"), vmem_limit_bytes=VMEM_LIMIT_BYTES),
        name="project",
    )(x, ctx, sc, sh, csc, csh, g, wn, wt, bg, cosn, sinn)


def _attn_kernel(q_ref, k_ref, v_ref, lam_ref, g_ref, o_ref, m_s, l_s, lfin_s, acc_s, s_s, mb_s, al_s, p_s, *,
                 tq, tk, nblk, ntile, qsub, qoff, unroll, lambda_init):
    nstep = ntile * nblk
    sw = q_ref.shape[-1]
    half = sw // 2

    m_s[...] = jnp.zeros(m_s.shape, F32)
    l_s[...] = jnp.zeros(l_s.shape, F32)
    lfin_s[...] = jnp.ones(lfin_s.shape, F32)
    acc_s[...] = jnp.zeros(acc_s.shape, F32)

    lv = lam_ref[...]
    lam = (jnp.exp(jnp.sum(lv[0:1] * lv[1:2], axis=1, keepdims=True))
           - jnp.exp(jnp.sum(lv[2:3] * lv[3:4], axis=1, keepdims=True)) + lambda_init)

    def values(u):
        kb = lax.rem(u, nblk)
        slot = lax.rem(lax.div(u, nblk), 2)
        pv = jnp.dot(v_ref[0, kb], p_s[...], preferred_element_type=F32)
        acc_s[slot] = al_s[...] * acc_s[slot] + pv

    def scale(u):
        kb = lax.rem(u, nblk)
        first = kb == 0
        m_old = jnp.where(first, -jnp.inf, m_s[...])
        l_old = jnp.where(first, 0.0, l_s[...])
        m_new = jnp.maximum(m_old, mb_s[...])
        alpha = jnp.exp2(m_old - m_new)
        psum = []
        for c in range(0, 2 * tq, 128):
            p = jnp.exp2(s_s[:, c:c + 128] - m_new[:, c:c + 128])
            psum.append(jnp.sum(p, axis=0, keepdims=True))
            p_s[:, c:c + 128] = p.astype(BF16)
        l_new = alpha * l_old + jnp.concatenate(psum, axis=1)
        l_s[...] = l_new
        lfin_s[...] = jnp.where(kb == nblk - 1, l_new, lfin_s[...])
        al_s[...] = alpha
        m_s[...] = m_new

    def scores(u):
        tile = lax.div(u, nblk)
        kb = lax.rem(u, nblk)
        qbd = jnp.concatenate([q_ref[0, 0, qoff + tile * qsub + c] for c in range(qsub)], axis=1)
        k = k_ref[0, pl.ds(pl.multiple_of(kb * tk, tk), tk), :]
        s = jnp.dot(k, qbd, preferred_element_type=F32)
        s_s[...] = s
        mb_s[...] = jnp.max(s, axis=0, keepdims=True)

    def finalize(tile):
        inv_l = 1.0 / lfin_s[...]
        acc = acc_s[lax.rem(tile, 2)]
        for c in range(qsub):
            a1, a2 = c * sw, c * sw + half
            o = acc[:, a1:a2] * inv_l[:, a1:a2] - lam * (acc[:, a2:a2 + half] * inv_l[:, a2:a2 + half])
            y = o * lax.rsqrt(jnp.mean(o * o, axis=0, keepdims=True) + EPS) * g_ref[...] * (1.0 - lambda_init)
            o_ref[0, pl.ds(pl.multiple_of(tile * tq + c * half, half), half), :] = y.T.astype(BF16)

    scores(0)
    scale(0)
    scores(1)

    def body(i, carry):
        u = unroll * i + 1
        for d in range(unroll):
            values(u - 1 + d)
            scale(u + d)
            scores(u + 1 + d)
        for d in range(unroll):
            w = u - 1 + d

            @pl.when(lax.rem(w, nblk) == nblk - 1)
            def _():
                finalize(lax.div(w, nblk))
        return carry

    assert (nstep - 2) % unroll == 0 and nblk >= unroll
    lax.fori_loop(0, (nstep - 2) // unroll, body, 0)
    values(nstep - 2)
    scale(nstep - 1)
    values(nstep - 1)
    finalize(ntile - 1)


def _attn_unroll(loop_steps, nblk):
    return max(u for u in range(1, min(ATT_MAX_UNROLL, nblk) + 1) if loop_steps % u == 0)


def _attention(qat, ka, vat, lamv, subg, t, lambda_init):
    bsz, _, nsub, _, sw = qat.shape
    n = ka.shape[1]
    tq, tk = ATT_TQ, ATT_TK
    sub = sw // 2
    nblk = n // tk
    kern = functools.partial(_attn_kernel, tq=tq, tk=tk, nblk=nblk, ntile=t // tq, qsub=tq // sub,
                             qoff=(n - t) // sub, unroll=_attn_unroll((t // tq) * nblk - 2, nblk), lambda_init=lambda_init)
    row = lambda: pltpu.VMEM((1, 2 * tq), F32)
    return pl.pallas_call(
        kern,
        grid=(bsz, DA_HEADS),
        in_specs=[
            pl.BlockSpec((1, 1, nsub, 128, sw), lambda b, h: (b, h, 0, 0, 0)),
            pl.BlockSpec((1, n, 128), lambda b, h: (b, 0, h)),
            pl.BlockSpec((1, nblk, 128, tk), lambda b, h: (b, 0, h, 0)),
            pl.BlockSpec(lamv.shape, lambda b, h: (0, 0)),
            pl.BlockSpec(subg.shape, lambda b, h: (0, 0)),
        ],
        out_specs=pl.BlockSpec((1, t, 128), lambda b, h: (b, 0, h)),
        out_shape=jax.ShapeDtypeStruct((bsz, t, DA_HEADS * DA_V), BF16),
        scratch_shapes=[row(), row(), row(), pltpu.VMEM((2, DA_V, 2 * tq), F32), pltpu.VMEM((tk, 2 * tq), F32),
                        row(), row(), pltpu.VMEM((tk, 2 * tq), BF16)],
        compiler_params=pltpu.CompilerParams(
            dimension_semantics=("arbitrary", "arbitrary"), vmem_limit_bytes=VMEM_LIMIT_BYTES),
        name="diff_attn",
    )(qat, ka, vat, lamv, subg)


def _mlstm_kernel(qf_ref, kf_ref, vf_ref, gf_ref, cf_ref, qb_ref, kb_ref, vb_ref, gb_ref, cb_ref,
                  hf_ref, hb_ref, c_s, m_s, *, L):
    j = pl.program_id(1)

    @pl.when(j == 0)
    def _():
        c_s[...] = jnp.zeros(c_s.shape, F32)
        m_s[...] = jnp.zeros(m_s.shape, F32)

    sidx = lax.broadcasted_iota(jnp.int32, (L, L), 0)
    tidx = lax.broadcasted_iota(jnp.int32, (L, L), 1)
    row128 = lax.broadcasted_iota(jnp.int32, (128, L), 0)
    lane128 = lax.broadcasted_iota(jnp.int32, (ML_STATE_ROWS, 128), 1)
    row16 = lax.broadcasted_iota(jnp.int32, (16, L), 0)

    chains = range(2 * ML_HEADS)
    rev = [r >= ML_HEADS for r in chains]
    head = [r % ML_HEADS for r in chains]
    refs = [(qb_ref, kb_ref, vb_ref, gb_ref, cb_ref) if rev[r] else (qf_ref, kf_ref, vf_ref, gf_ref, cf_ref)
            for r in chains]
    pairs = [slice((head[r] // 2) * 128, (head[r] // 2 + 1) * 128) for r in chains]
    odd = [head[r] % 2 == 1 for r in chains]

    qpair = [refs[r][0][0, pairs[r], :] for r in chains]
    kpair = [refs[r][1][0, :, pairs[r]] for r in chains]
    vt = [refs[r][2][0, head[r] * ML_V:(head[r] + 1) * ML_V, :] for r in chains]
    b_row = [refs[r][3][0, r:r + 1, :] for r in chains]
    c_row = [refs[r][3][0, 8 + r:9 + r, :] for r in chains]
    cm_row = [refs[r][3][0, 16 + r:17 + r, :] for r in chains]
    c_col = [refs[r][4][0, :, r:r + 1] for r in chains]
    m_prev = [m_s[r] for r in chains]
    state = [c_s[r] for r in chains]
    m_run = [jnp.maximum(m_prev[r], cm_row[r]) for r in chains]

    qt = [jnp.where((row128 >= ML_QK) if odd[r] else (row128 < ML_QK), qpair[r], jnp.zeros_like(qpair[r]))
          for r in chains]
    st = [jnp.dot(kpair[r], qt[r], preferred_element_type=F32) for r in chains]
    cq = [jnp.dot(state[r].astype(BF16), qpair[r], preferred_element_type=F32) for r in chains]
    sp = [jnp.where((sidx >= tidx) if rev[r] else (sidx <= tidx),
                    st[r] * jnp.exp2(c_col[r] - m_run[r]), 0.0) for r in chains]
    den_intra = [jnp.sum(sp[r], axis=0, keepdims=True) for r in chains]
    num_intra = [jnp.dot(vt[r], sp[r].astype(BF16), preferred_element_type=F32) for r in chains]

    b_tot = [b_row[r][:, 0:1] if rev[r] else b_row[r][:, L - 1:L] for r in chains]
    cm_end = [cm_row[r][:, 0:1] if rev[r] else cm_row[r][:, L - 1:L] for r in chains]
    m_end = [jnp.maximum(m_prev[r][:, 0:1], cm_end[r]) for r in chains]
    w_row = [jnp.exp2(c_row[r] - m_end[r]) for r in chains]
    aug = [jnp.concatenate([vt[r].astype(F32) * w_row[r],
                            jnp.where(row16 == 0, jnp.broadcast_to(w_row[r], (16, L)), 0.0)], axis=0)
           for r in chains]
    upd = [jnp.dot(aug[r].astype(BF16), kpair[r], preferred_element_type=F32) for r in chains]

    for r in chains:
        w_inter = jnp.exp2(m_prev[r] - m_run[r])
        num = num_intra[r] + w_inter * cq[r][0:ML_V]
        den = den_intra[r] + w_inter * cq[r][ML_V:ML_V + 1]
        ht = num * (1.0 / jnp.maximum(jnp.abs(den), jnp.exp2(-(b_row[r] + m_run[r]))))
        out_ref = hb_ref if rev[r] else hf_ref
        out_ref[0, :, head[r] * ML_V:(head[r] + 1) * ML_V] = ht.T
        own_lanes = (lane128 >= ML_QK) if odd[r] else (lane128 < ML_QK)
        c_s[r] = jnp.exp2(m_prev[r][:, 0:1] - m_end[r]) * state[r] + jnp.where(own_lanes, upd[r], 0.0)
        m_s[r] = jnp.broadcast_to(b_tot[r] + m_end[r], (1, L))


def _mlstm(qmt, km, vmt, gt, cc, t):
    bsz, _, n = qmt.shape
    L = ML_CHUNK_LEN
    nch = n // L
    nctx_ch = (n - t) // L
    nlat = t // L

    def fwd_c(j):
        return j

    def bwd_c(j):
        return jnp.where(j < nctx_ch, nctx_ch - 1 - j, nch - 1 - (j - nctx_ch))

    def specs(cfn):
        return [
            pl.BlockSpec((1, 256, L), lambda b, j: (b, 0, cfn(j))),
            pl.BlockSpec((1, L, 256), lambda b, j: (b, cfn(j), 0)),
            pl.BlockSpec((1, 512, L), lambda b, j: (b, 0, cfn(j))),
            pl.BlockSpec((1, 24, L), lambda b, j: (b, 0, cfn(j))),
            pl.BlockSpec((1, L, 128), lambda b, j: (b, cfn(j), 0)),
        ]

    lat = lambda j: jnp.maximum(j - nctx_ch, 0)
    out_specs = [
        pl.BlockSpec((1, L, 512), lambda b, j: (b, lat(j), 0)),
        pl.BlockSpec((1, L, 512), lambda b, j: (b, nlat - 1 - lat(j), 0)),
    ]
    out_shape = [jax.ShapeDtypeStruct((bsz, t, 512), F32)] * 2
    return pl.pallas_call(
        functools.partial(_mlstm_kernel, L=L),
        grid=(bsz, nch),
        in_specs=specs(fwd_c) + specs(bwd_c),
        out_specs=out_specs,
        out_shape=out_shape,
        scratch_shapes=[pltpu.VMEM((2 * ML_HEADS, ML_STATE_ROWS, 128), F32),
                        pltpu.VMEM((2 * ML_HEADS, 1, L), F32)],
        compiler_params=pltpu.CompilerParams(
            dimension_semantics=("arbitrary", "arbitrary"), vmem_limit_bytes=VMEM_LIMIT_BYTES),
        name="mlstm",
    )(qmt, km, vmt, gt, cc, qmt, km, vmt, gt, cc)


def _final_kernel(x_ref, da_ref, hf_ref, hb_ref, mo_ref, g1_ref, sh2_ref, sc2_ref, g2_ref,
                  mlg_ref, n2g_ref, fg_ref, wo_ref, w1_ref, w2_ref, o_ref):
    x = x_ref[0]
    hsum = hf_ref[0] + hb_ref[0]
    parts = []
    for h in range(ML_HEADS):
        u = hsum[:, h * ML_V:(h + 1) * ML_V]
        parts.append(u * lax.rsqrt(jnp.mean(u * u, axis=1, keepdims=True) + EPS))
    ml = jnp.concatenate(parts, axis=1) * mlg_ref[...] * mo_ref[0].astype(F32)
    cat = jnp.concatenate([da_ref[0], ml.astype(BF16)], axis=1)
    y = jnp.dot(cat, wo_ref[...], preferred_element_type=F32)
    x1 = x + g1_ref[0] * y
    xn = (_rms(x1, n2g_ref[...]) * (1.0 + sc2_ref[0]) + sh2_ref[0]).astype(BF16)
    hid = jnp.dot(xn, w1_ref[...], preferred_element_type=F32)
    hid = jnp.square(jnp.maximum(hid, 0.0)).astype(BF16)
    x2 = x1 + g2_ref[0] * jnp.dot(hid, w2_ref[...], preferred_element_type=F32)
    o_ref[0] = _rms(x2, fg_ref[...])


def _final(x, da, hf, hb, mo, g1, sh2, sc2, g2, mlg, n2g, fg, wo, w1, w2):
    bsz, t, d = x.shape
    tm = FINAL_TILE
    tok = lambda w: pl.BlockSpec((1, tm, w), lambda b, i: (b, i, 0))
    mod = pl.BlockSpec((1, 1, d), lambda b, i: (b, 0, 0))
    const = lambda a: pl.BlockSpec(a.shape, lambda b, i: (0, 0), pipeline_mode=pl.Buffered(1))
    return pl.pallas_call(
        _final_kernel,
        grid=(bsz, t // tm),
        in_specs=[tok(d), tok(512), tok(512), tok(512), tok(512),
                  mod, mod, mod, mod, const(mlg), const(n2g), const(fg), const(wo), const(w1), const(w2)],
        out_specs=tok(d),
        out_shape=jax.ShapeDtypeStruct((bsz, t, d), F32),
        compiler_params=pltpu.CompilerParams(
            dimension_semantics=("arbitrary", "arbitrary"), vmem_limit_bytes=VMEM_LIMIT_BYTES),
        name="final",
    )(x, da, hf, hb, mo, g1, sh2, sc2, g2, mlg, n2g, fg, wo, w1, w2)


def _rope_tables(t, nctx):
    rows = t // GRID_W
    row = jnp.repeat(jnp.arange(rows, dtype=F32), GRID_W)
    col = jnp.tile(jnp.arange(GRID_W, dtype=F32), rows)
    half = DA_QK // 2
    inv = ROPE_BASE ** (-jnp.arange(0, half, 2, dtype=F32) / half)
    ar = row[:, None] * inv
    ac = col[:, None] * inv
    ang = jnp.concatenate([ar, ar, ac, ac], axis=-1)
    sign = jnp.where(jnp.arange(DA_QK) % 32 < 16, -1.0, 1.0).astype(F32)
    cos = jnp.concatenate([jnp.ones((nctx, DA_QK), F32), jnp.cos(ang)], axis=0)
    sin = jnp.concatenate([jnp.zeros((nctx, DA_QK), F32), jnp.sin(ang) * sign], axis=0)
    cos = jnp.tile(cos, (1, 2))
    sin = jnp.tile(sin, (1, 2))
    return cos, sin


def _split_weights(w_in_l, b_gate_l):
    nq = DA_HEADS * 2 * DA_QK
    o = 0
    wdq = w_in_l[:, o:o + nq]; o += nq
    wdk = w_in_l[:, o:o + nq]; o += nq
    wdv = w_in_l[:, o:o + DA_HEADS * DA_V]; o += DA_HEADS * DA_V
    wmq = w_in_l[:, o:o + ML_HEADS * ML_QK]; o += ML_HEADS * ML_QK
    wmk = w_in_l[:, o:o + ML_HEADS * ML_QK]; o += ML_HEADS * ML_QK
    wmv = w_in_l[:, o:o + ML_HEADS * ML_V]; o += ML_HEADS * ML_V
    wmo = w_in_l[:, o:o + ML_HEADS * ML_V]; o += ML_HEADS * ML_V
    wmg = w_in_l[:, o:o + 4 * ML_HEADS]
    perm = jnp.array([0, 1, 2, 3, 8, 9, 10, 11, 4, 5, 6, 7, 12, 13, 14, 15])
    wmg = wmg[:, perm]
    bg = b_gate_l[perm].reshape(16, 1).astype(F32)
    qscale = DA_QK ** -0.5 * LOG2_E
    kscale = ML_QK ** -0.5
    wn = jnp.concatenate([wdk, wmk * kscale, wmo], axis=1).astype(BF16)
    wt = jnp.concatenate([wdq * qscale, wdv, wmq, wmv, wmg], axis=1).T.astype(BF16)
    return wn, wt, bg


def kernel(x, c, ctx, c_ctx, w_ada, b_ada, norm1_g, norm2_g, w_in, b_gate, lam_q1, lam_k1, lam_q2, lam_k2,
           subln_g, mlstm_norm_g, w_out, w_fc1, w_fc2, final_g):
    bsz, t, d = x.shape
    nctx = ctx.shape[1]
    depth = w_ada.shape[0]
    assert depth == 1, "single-layer block: the context stream is never updated"
    assert TOK_TILE == ML_CHUNK_LEN, "the projection computes per-chunk gate sums on its own tile"
    assert nctx % TOK_TILE == 0 and ATT_TQ % TOK_TILE == 0 and ATT_TK % TOK_TILE == 0
    assert t % ATT_TQ == 0 and t % ML_CHUNK_LEN == 0 and (nctx + t) % ATT_TK == 0 and t % GRID_W == 0
    assert t % FINAL_TILE == 0
    assert (t // ATT_TQ) * ((nctx + t) // ATT_TK) >= 3, "the attention pipeline needs at least three steps"
    lambda_init = 0.2

    cc = jnp.concatenate([c, c_ctx[None, :], jnp.zeros((8 - bsz - 1, d), F32)], axis=0)
    mod = _adaln(cc, w_ada[0], b_ada[0][None, :])
    mb = mod[:bsz].reshape(bsz, 1, 6, d)
    sh1, sc1, g1, sh2, sc2, g2 = [mb[:, :, k] for k in range(6)]
    mc = mod[bsz:bsz + 1].reshape(1, 1, 6, d)
    csh1, csc1 = mc[:, :, 0], mc[:, :, 1]

    wn, wt, bg = _split_weights(w_in[0], b_gate[0])
    cosn, sinn = _rope_tables(t, nctx)
    ka, km, mo, qat, vat, qmt, vmt, gt, cc = _project(
        x, ctx, sc1, sh1, csc1, csh1, norm1_g[0][None, :], wn, wt, bg, cosn, sinn)

    lamv = jnp.stack([lam_q1[0], lam_k1[0], lam_q2[0], lam_k2[0]]).astype(F32)
    da = _attention(qat, ka, vat, lamv, subln_g[0].reshape(DA_V, 1), t, lambda_init)
    hf, hb = _mlstm(qmt, km, vmt, gt, cc, t)

    return _final(x, da, hf, hb, mo, g1, sh2, sc2, g2, mlstm_norm_g[0][None, :], norm2_g[0][None, :],
                  final_g[None, :], w_out[0].astype(BF16), w_fc1[0].astype(BF16), w_fc2[0].astype(BF16))
```

```python
import functools

import jax
import jax.numpy as jnp
from jax import lax
from jax.experimental import pallas as pl
from jax.experimental.pallas import tpu as pltpu

DA_HEADS = 4
DA_QK = 64
DA_V = 128
ML_HEADS = 4
ML_QK = 64
ML_V = 128
GRID_W = 64
ROPE_BASE = 10000.0
EPS = 1e-6
LOG2_E = 1.4426950408889634
LANES = 128

TOK_TILE = 256
FINAL_TILE = 512
ATT_TQ = 512
ATT_TK = 768
ATT_MAX_UNROLL = 2
ML_CHUNK_LEN = 256
ML_STATE_ROWS = 144
VMEM_LIMIT_BYTES = 56 * 1024 * 1024

F32 = jnp.float32
BF16 = jnp.bfloat16


def _rms(x, g):
    return x * lax.rsqrt(jnp.mean(x * x, axis=-1, keepdims=True) + EPS) * g


def _adaln_kernel(c_ref, w_ref, b_ref, o_ref):
    c = c_ref[...]
    s = (c * jax.nn.sigmoid(c)).astype(BF16)
    o_ref[...] = jnp.dot(s, w_ref[...].astype(BF16), preferred_element_type=F32) + b_ref[...]


def _adaln(cc, w, b):
    rows, d = cc.shape
    n = w.shape[1]
    bn = d
    return pl.pallas_call(
        _adaln_kernel,
        grid=(n // bn,),
        in_specs=[pl.BlockSpec((rows, d), lambda j: (0, 0)),
                  pl.BlockSpec((d, bn), lambda j: (0, j)),
                  pl.BlockSpec((1, bn), lambda j: (0, j))],
        out_specs=pl.BlockSpec((rows, bn), lambda j: (0, j)),
        out_shape=jax.ShapeDtypeStruct((rows, n), F32),
        name="adaln",
    )(cc, w, b)


def _proj_kernel(x_ref, ctx_ref, sc_ref, sh_ref, csc_ref, csh_ref, g_ref, wn_ref, wt_ref, bg_ref,
                 cosn_ref, sinn_ref,
                 ka_ref, km_ref, mo_ref, qat_ref, vat_ref, qmt_ref, vmt_ref, gt_ref, cc_ref, *, nct):
    i = pl.program_id(0)
    is_ctx = i < nct
    x = jnp.where(is_ctx, ctx_ref[0], x_ref[0])
    sc = jnp.where(is_ctx, csc_ref[0], sc_ref[0])
    sh = jnp.where(is_ctx, csh_ref[0], sh_ref[0])
    xm = _rms(x, g_ref[...]) * (1.0 + sc) + sh
    xb = xm.astype(BF16)
    nt = (((1,), (1,)), ((), ()))
    hd = DA_HEADS * 2 * DA_QK
    ngate = 2 * hd + 768

    g = lax.dot_general(wt_ref[ngate:ngate + 16, :], xb, nt, preferred_element_type=F32) + bg_ref[...]
    li8 = g[0:8]
    lf8 = jnp.minimum(g[8:16], 0.0) - jnp.log1p(jnp.exp(-jnp.abs(g[8:16])))
    tm = g.shape[1]
    row8 = lax.broadcasted_iota(jnp.int32, (8, tm), 0)
    lane8 = lax.broadcasted_iota(jnp.int32, (8, tm), 1)
    is_f = row8 < ML_HEADS

    def scan(x, op, fill):
        step = 1
        while step < tm:
            prev = jnp.where(lane8 >= step, pltpu.roll(x, step, axis=1), fill)
            nxt = jnp.where(lane8 < tm - step, pltpu.roll(x, tm - step, axis=1), fill)
            x = op(x, jnp.where(is_f, prev, nxt))
            step *= 2
        return x

    b8 = scan(lf8, jnp.add, 0.0)
    c8 = li8 - b8
    cm8 = scan(c8, jnp.maximum, -jnp.inf)
    gt_ref[0] = jnp.concatenate([b8, c8, cm8], axis=0) * LOG2_E
    cc_ref[0] = (jnp.concatenate([c8, jnp.zeros((120, tm), F32)], axis=0) * LOG2_E).T

    hn = jnp.dot(xb, wn_ref[...], preferred_element_type=F32)
    ht = lax.dot_general(wt_ref[0:ngate, :], xb, nt, preferred_element_type=F32)

    cosn, sinn = cosn_ref[...], sinn_ref[...]
    cost, sint = cosn.T, sinn.T
    lane = lax.broadcasted_iota(jnp.int32, (tm, 128), 1)
    low_half = (lane % 32) < 16
    for h in range(DA_HEADS):
        lo, hi = h * 128, (h + 1) * 128
        k = hn[:, lo:hi]
        kswap = jnp.where(low_half, pltpu.roll(k, 128 - 16, axis=1), pltpu.roll(k, 16, axis=1))
        ka_ref[0, :, lo:hi] = (k * cosn + kswap * sinn).astype(BF16)
        q = ht[lo:hi]
        qswap = jnp.concatenate([q[r0 + 16:r0 + 32] if part == 0 else q[r0:r0 + 16]
                                 for r0 in range(0, 128, 32) for part in (0, 1)], axis=0)
        q = (q * cost + qswap * sint).astype(BF16)
        qrow = lax.broadcasted_iota(jnp.int32, q.shape, 0)
        zero = jnp.zeros_like(q)
        qat_ref[0, h, 0] = jnp.concatenate(
            [jnp.where(qrow < DA_QK, q, zero), jnp.where(qrow >= DA_QK, q, zero)], axis=1)
    km_ref[0] = hn[:, hd:hd + 256].astype(BF16)
    mo_ref[0] = jax.nn.sigmoid(hn[:, hd + 256:hd + 768]).astype(BF16)

    vat_ref[0, 0] = ht[hd:2 * hd].astype(BF16)
    qmt_ref[0] = ht[2 * hd:2 * hd + 256].astype(BF16)
    vmt_ref[0] = ht[2 * hd + 256:2 * hd + 768].astype(BF16)


def _project(x, ctx, sc, sh, csc, csh, g, wn, wt, bg, cosn, sinn):
    bsz, t, d = x.shape
    nctx = ctx.shape[1]
    n = nctx + t
    tm = TOK_TILE
    nct = nctx // tm
    r = ATT_TK // tm
    hd = DA_HEADS * 2 * DA_QK
    const2 = lambda i, b: (0, 0)
    const3 = lambda i, b: (0, 0, 0)
    in_specs = [
        pl.BlockSpec((1, tm, d), lambda i, b: (b, jnp.maximum(i - nct, 0), 0)),
        pl.BlockSpec((1, tm, d), lambda i, b: (b, jnp.minimum(i, nct - 1), 0)),
        pl.BlockSpec((1, 1, d), lambda i, b: (b, 0, 0)),
        pl.BlockSpec((1, 1, d), lambda i, b: (b, 0, 0)),
        pl.BlockSpec((1, 1, d), const3),
        pl.BlockSpec((1, 1, d), const3),
        pl.BlockSpec((1, d), const2),
        pl.BlockSpec(wn.shape, const2),
        pl.BlockSpec(wt.shape, const2),
        pl.BlockSpec(bg.shape, const2),
        pl.BlockSpec((tm, 128), lambda i, b: (i, 0)),
        pl.BlockSpec((tm, 128), lambda i, b: (i, 0)),
    ]
    out_specs = [
        pl.BlockSpec((1, tm, hd), lambda i, b: (b, i, 0)),
        pl.BlockSpec((1, tm, 256), lambda i, b: (b, i, 0)),
        pl.BlockSpec((1, tm, 512), lambda i, b: (b, jnp.where(i < nct, n // tm - nct + i, i - nct), 0)),
        pl.BlockSpec((1, DA_HEADS, 1, 128, 2 * tm), lambda i, b: (b, 0, i, 0, 0)),
        pl.BlockSpec((1, 1, 512, tm), lambda i, b: (b, i // r, 0, i % r)),
        pl.BlockSpec((1, 256, tm), lambda i, b: (b, 0, i)),
        pl.BlockSpec((1, 512, tm), lambda i, b: (b, 0, i)),
        pl.BlockSpec((1, 24, tm), lambda i, b: (b, 0, i)),
        pl.BlockSpec((1, tm, 128), lambda i, b: (b, i, 0)),
    ]
    out_shape = [
        jax.ShapeDtypeStruct((bsz, n, hd), BF16),
        jax.ShapeDtypeStruct((bsz, n, 256), BF16),
        jax.ShapeDtypeStruct((bsz, n, 512), BF16),
        jax.ShapeDtypeStruct((bsz, DA_HEADS, n // tm, 128, 2 * tm), BF16),
        jax.ShapeDtypeStruct((bsz, n // ATT_TK, 512, ATT_TK), BF16),
        jax.ShapeDtypeStruct((bsz, 256, n), BF16),
        jax.ShapeDtypeStruct((bsz, 512, n), BF16),
        jax.ShapeDtypeStruct((bsz, 24, n), F32),
        jax.ShapeDtypeStruct((bsz, n, 128), F32),
    ]
    return pl.pallas_call(
        functools.partial(_proj_kernel, nct=nct),
        grid=(n // tm, bsz),
        in_specs=in_specs,
        out_specs=out_specs,
        out_shape=out_shape,
        compiler_params=pltpu.CompilerParams(
            dimension_semantics=("arbitrary", "arbitrary"), vmem_limit_bytes=VMEM_LIMIT_BYTES),
        name="project",
    )(x, ctx, sc, sh, csc, csh, g, wn, wt, bg, cosn, sinn)


def _attn_kernel(q_ref, k_ref, v_ref, lam_ref, g_ref, o_ref, m_s, l_s, lfin_s, acc_s, s_s, mb_s, al_s, p_s, *,
                 tq, tk, nblk, ntile, qsub, qoff, unroll, lambda_init):
    nstep = ntile * nblk
    sw = q_ref.shape[-1]
    half = sw // 2

    m_s[...] = jnp.zeros(m_s.shape, F32)
    l_s[...] = jnp.zeros(l_s.shape, F32)
    lfin_s[...] = jnp.ones(lfin_s.shape, F32)
    acc_s[...] = jnp.zeros(acc_s.shape, F32)

    lv = lam_ref[...]
    lam = (jnp.exp(jnp.sum(lv[0:1] * lv[1:2], axis=1, keepdims=True))
           - jnp.exp(jnp.sum(lv[2:3] * lv[3:4], axis=1, keepdims=True)) + lambda_init)

    def values(u):
        kb = lax.rem(u, nblk)
        slot = lax.rem(lax.div(u, nblk), 2)
        pv = jnp.dot(v_ref[0, kb], p_s[:, 0:2 * tq], preferred_element_type=F32)
        acc_s[slot] = al_s[...] * acc_s[slot] + pv

    def scale(u):
        kb = lax.rem(u, nblk)
        first = kb == 0
        m_old = jnp.where(first, -jnp.inf, m_s[...])
        l_old = jnp.where(first, 0.0, l_s[...])
        m_new = jnp.maximum(m_old, mb_s[...])
        alpha = jnp.exp2(m_old - m_new)
        p = jnp.exp2(s_s[:, 0:2 * tq] - m_new)
        l_new = alpha * l_old + jnp.sum(p, axis=0, keepdims=True)
        l_s[...] = l_new
        lfin_s[...] = jnp.where(kb == nblk - 1, l_new, lfin_s[...])
        p_s[:, 0:2 * tq] = p.astype(BF16)
        al_s[...] = alpha
        m_s[...] = m_new

    def scores(u):
        tile = lax.div(u, nblk)
        kb = lax.rem(u, nblk)
        qbd = jnp.concatenate([q_ref[0, 0, qoff + tile * qsub + c] for c in range(qsub)], axis=1)
        k = k_ref[0, pl.ds(pl.multiple_of(kb * tk, tk), tk), :]
        s = jnp.dot(k, qbd, preferred_element_type=F32)
        s_s[:, 0:2 * tq] = s
        mb_s[...] = jnp.max(s, axis=0, keepdims=True)

    def finalize(tile):
        inv_l = 1.0 / lfin_s[...]
        acc = acc_s[lax.rem(tile, 2)]
        for c in range(qsub):
            a1, a2 = c * sw, c * sw + half
            o = acc[:, a1:a2] * inv_l[:, a1:a2] - lam * (acc[:, a2:a2 + half] * inv_l[:, a2:a2 + half])
            y = o * lax.rsqrt(jnp.mean(o * o, axis=0, keepdims=True) + EPS) * g_ref[...] * (1.0 - lambda_init)
            o_ref[0, pl.ds(pl.multiple_of(tile * tq + c * half, half), half), :] = y.T.astype(BF16)

    scores(0)
    scale(0)
    scores(1)

    def body(i, carry):
        u = unroll * i + 1
        for d in range(unroll):
            values(u - 1 + d)
            scale(u + d)
            scores(u + 1 + d)
        for d in range(unroll):
            w = u - 1 + d

            @pl.when(lax.rem(w, nblk) == nblk - 1)
            def _():
                finalize(lax.div(w, nblk))
        return carry

    assert (nstep - 2) % unroll == 0 and nblk >= unroll
    lax.fori_loop(0, (nstep - 2) // unroll, body, 0)
    values(nstep - 2)
    scale(nstep - 1)
    values(nstep - 1)
    finalize(ntile - 1)


def _attn_unroll(loop_steps, nblk):
    return max(u for u in range(1, min(ATT_MAX_UNROLL, nblk) + 1) if loop_steps % u == 0)


def _attention(qat, ka, vat, lamv, subg, t, lambda_init):
    bsz, _, nsub, _, sw = qat.shape
    n = ka.shape[1]
    tq, tk = ATT_TQ, ATT_TK
    sub = sw // 2
    nblk = n // tk
    kern = functools.partial(_attn_kernel, tq=tq, tk=tk, nblk=nblk, ntile=t // tq, qsub=tq // sub,
                             qoff=(n - t) // sub, unroll=_attn_unroll((t // tq) * nblk - 2, nblk), lambda_init=lambda_init)
    row = lambda: pltpu.VMEM((1, 2 * tq), F32)
    return pl.pallas_call(
        kern,
        grid=(bsz, DA_HEADS),
        in_specs=[
            pl.BlockSpec((1, 1, nsub, 128, sw), lambda b, h: (b, h, 0, 0, 0)),
            pl.BlockSpec((1, n, 128), lambda b, h: (b, 0, h)),
            pl.BlockSpec((1, nblk, 128, tk), lambda b, h: (b, 0, h, 0)),
            pl.BlockSpec(lamv.shape, lambda b, h: (0, 0)),
            pl.BlockSpec(subg.shape, lambda b, h: (0, 0)),
        ],
        out_specs=pl.BlockSpec((1, t, 128), lambda b, h: (b, 0, h)),
        out_shape=jax.ShapeDtypeStruct((bsz, t, DA_HEADS * DA_V), BF16),
        scratch_shapes=[row(), row(), row(), pltpu.VMEM((2, DA_V, 2 * tq), F32),
                        pltpu.VMEM((tk, 2 * tq + LANES), F32),
                        row(), row(), pltpu.VMEM((tk, 2 * tq + LANES), BF16)],
        compiler_params=pltpu.CompilerParams(
            dimension_semantics=("arbitrary", "arbitrary"), vmem_limit_bytes=VMEM_LIMIT_BYTES),
        name="diff_attn",
    )(qat, ka, vat, lamv, subg)


def _mlstm_kernel(qf_ref, kf_ref, vf_ref, gf_ref, cf_ref, qb_ref, kb_ref, vb_ref, gb_ref, cb_ref,
                  hf_ref, hb_ref, c_s, m_s, *, L):
    j = pl.program_id(1)

    @pl.when(j == 0)
    def _():
        c_s[...] = jnp.zeros(c_s.shape, F32)
        m_s[...] = jnp.zeros(m_s.shape, F32)

    sidx = lax.broadcasted_iota(jnp.int32, (L, L), 0)
    tidx = lax.broadcasted_iota(jnp.int32, (L, L), 1)
    row128 = lax.broadcasted_iota(jnp.int32, (128, L), 0)
    lane128 = lax.broadcasted_iota(jnp.int32, (ML_STATE_ROWS, 128), 1)
    row16 = lax.broadcasted_iota(jnp.int32, (16, L), 0)

    chains = range(2 * ML_HEADS)
    rev = [r >= ML_HEADS for r in chains]
    head = [r % ML_HEADS for r in chains]
    refs = [(qb_ref, kb_ref, vb_ref, gb_ref, cb_ref) if rev[r] else (qf_ref, kf_ref, vf_ref, gf_ref, cf_ref)
            for r in chains]
    pairs = [slice((head[r] // 2) * 128, (head[r] // 2 + 1) * 128) for r in chains]
    odd = [head[r] % 2 == 1 for r in chains]

    qpair = [refs[r][0][0, pairs[r], :] for r in chains]
    kpair = [refs[r][1][0, :, pairs[r]] for r in chains]
    vt = [refs[r][2][0, head[r] * ML_V:(head[r] + 1) * ML_V, :] for r in chains]
    b_row = [refs[r][3][0, r:r + 1, :] for r in chains]
    c_row = [refs[r][3][0, 8 + r:9 + r, :] for r in chains]
    cm_row = [refs[r][3][0, 16 + r:17 + r, :] for r in chains]
    c_col = [refs[r][4][0, :, r:r + 1] for r in chains]
    m_prev = [m_s[r] for r in chains]
    state = [c_s[r] for r in chains]
    m_run = [jnp.maximum(m_prev[r], cm_row[r]) for r in chains]

    qt = [jnp.where((row128 >= ML_QK) if odd[r] else (row128 < ML_QK), qpair[r], jnp.zeros_like(qpair[r]))
          for r in chains]
    st = [jnp.dot(kpair[r], qt[r], preferred_element_type=F32) for r in chains]
    cq = [jnp.dot(state[r].astype(BF16), qpair[r], preferred_element_type=F32) for r in chains]
    sp = [jnp.where((sidx >= tidx) if rev[r] else (sidx <= tidx),
                    st[r] * jnp.exp2(c_col[r] - m_run[r]), 0.0) for r in chains]
    den_intra = [jnp.sum(sp[r], axis=0, keepdims=True) for r in chains]
    num_intra = [jnp.dot(vt[r], sp[r].astype(BF16), preferred_element_type=F32) for r in chains]

    b_tot = [b_row[r][:, 0:1] if rev[r] else b_row[r][:, L - 1:L] for r in chains]
    cm_end = [cm_row[r][:, 0:1] if rev[r] else cm_row[r][:, L - 1:L] for r in chains]
    m_end = [jnp.maximum(m_prev[r][:, 0:1], cm_end[r]) for r in chains]
    w_row = [jnp.exp2(c_row[r] - m_end[r]) for r in chains]
    aug = [jnp.concatenate([vt[r].astype(F32) * w_row[r],
                            jnp.where(row16 == 0, jnp.broadcast_to(w_row[r], (16, L)), 0.0)], axis=0)
           for r in chains]
    upd = [jnp.dot(aug[r].astype(BF16), kpair[r], preferred_element_type=F32) for r in chains]

    for r in chains:
        w_inter = jnp.exp2(m_prev[r] - m_run[r])
        num = num_intra[r] + w_inter * cq[r][0:ML_V]
        den = den_intra[r] + w_inter * cq[r][ML_V:ML_V + 1]
        ht = num * (1.0 / jnp.maximum(jnp.abs(den), jnp.exp2(-(b_row[r] + m_run[r]))))
        out_ref = hb_ref if rev[r] else hf_ref
        out_ref[0, :, head[r] * ML_V:(head[r] + 1) * ML_V] = ht.T
        own_lanes = (lane128 >= ML_QK) if odd[r] else (lane128 < ML_QK)
        c_s[r] = jnp.exp2(m_prev[r][:, 0:1] - m_end[r]) * state[r] + jnp.where(own_lanes, upd[r], 0.0)
        m_s[r] = jnp.broadcast_to(b_tot[r] + m_end[r], (1, L))


def _mlstm(qmt, km, vmt, gt, cc, t):
    bsz, _, n = qmt.shape
    L = ML_CHUNK_LEN
    nch = n // L
    nctx_ch = (n - t) // L
    nlat = t // L

    def fwd_c(j):
        return j

    def bwd_c(j):
        return jnp.where(j < nctx_ch, nctx_ch - 1 - j, nch - 1 - (j - nctx_ch))

    def specs(cfn):
        return [
            pl.BlockSpec((1, 256, L), lambda b, j: (b, 0, cfn(j))),
            pl.BlockSpec((1, L, 256), lambda b, j: (b, cfn(j), 0)),
            pl.BlockSpec((1, 512, L), lambda b, j: (b, 0, cfn(j))),
            pl.BlockSpec((1, 24, L), lambda b, j: (b, 0, cfn(j))),
            pl.BlockSpec((1, L, 128), lambda b, j: (b, cfn(j), 0)),
        ]

    lat = lambda j: jnp.maximum(j - nctx_ch, 0)
    out_specs = [
        pl.BlockSpec((1, L, 512), lambda b, j: (b, lat(j), 0)),
        pl.BlockSpec((1, L, 512), lambda b, j: (b, nlat - 1 - lat(j), 0)),
    ]
    out_shape = [jax.ShapeDtypeStruct((bsz, t, 512), F32)] * 2
    return pl.pallas_call(
        functools.partial(_mlstm_kernel, L=L),
        grid=(bsz, nch),
        in_specs=specs(fwd_c) + specs(bwd_c),
        out_specs=out_specs,
        out_shape=out_shape,
        scratch_shapes=[pltpu.VMEM((2 * ML_HEADS, ML_STATE_ROWS, 128), F32),
                        pltpu.VMEM((2 * ML_HEADS, 1, L), F32)],
        compiler_params=pltpu.CompilerParams(
            dimension_semantics=("arbitrary", "arbitrary"), vmem_limit_bytes=VMEM_LIMIT_BYTES),
        name="mlstm",
    )(qmt, km, vmt, gt, cc, qmt, km, vmt, gt, cc)


def _final_kernel(x_ref, da_ref, hf_ref, hb_ref, mo_ref, g1_ref, sh2_ref, sc2_ref, g2_ref,
                  mlg_ref, n2g_ref, fg_ref, wo_ref, w1_ref, w2_ref, o_ref):
    x = x_ref[0]
    hsum = hf_ref[0] + hb_ref[0]
    parts = []
    for h in range(ML_HEADS):
        u = hsum[:, h * ML_V:(h + 1) * ML_V]
        parts.append(u * lax.rsqrt(jnp.mean(u * u, axis=1, keepdims=True) + EPS))
    ml = jnp.concatenate(parts, axis=1) * mlg_ref[...] * mo_ref[0].astype(F32)
    cat = jnp.concatenate([da_ref[0], ml.astype(BF16)], axis=1)
    y = jnp.dot(cat, wo_ref[...], preferred_element_type=F32)
    x1 = x + g1_ref[0] * y
    xn = (_rms(x1, n2g_ref[...]) * (1.0 + sc2_ref[0]) + sh2_ref[0]).astype(BF16)
    hid = jnp.dot(xn, w1_ref[...], preferred_element_type=F32)
    hid = jnp.square(jnp.maximum(hid, 0.0)).astype(BF16)
    x2 = x1 + g2_ref[0] * jnp.dot(hid, w2_ref[...], preferred_element_type=F32)
    o_ref[0] = _rms(x2, fg_ref[...])


def _final(x, da, hf, hb, mo, g1, sh2, sc2, g2, mlg, n2g, fg, wo, w1, w2):
    bsz, t, d = x.shape
    tm = FINAL_TILE
    tok = lambda w: pl.BlockSpec((1, tm, w), lambda b, i: (b, i, 0))
    mod = pl.BlockSpec((1, 1, d), lambda b, i: (b, 0, 0))
    const = lambda a: pl.BlockSpec(a.shape, lambda b, i: (0, 0), pipeline_mode=pl.Buffered(1))
    return pl.pallas_call(
        _final_kernel,
        grid=(bsz, t // tm),
        in_specs=[tok(d), tok(512), tok(512), tok(512), tok(512),
                  mod, mod, mod, mod, const(mlg), const(n2g), const(fg), const(wo), const(w1), const(w2)],
        out_specs=tok(d),
        out_shape=jax.ShapeDtypeStruct((bsz, t, d), F32),
        compiler_params=pltpu.CompilerParams(
            dimension_semantics=("arbitrary", "arbitrary"), vmem_limit_bytes=VMEM_LIMIT_BYTES),
        name="final",
    )(x, da, hf, hb, mo, g1, sh2, sc2, g2, mlg, n2g, fg, wo, w1, w2)


def _rope_tables(t, nctx):
    rows = t // GRID_W
    row = jnp.repeat(jnp.arange(rows, dtype=F32), GRID_W)
    col = jnp.tile(jnp.arange(GRID_W, dtype=F32), rows)
    half = DA_QK // 2
    inv = ROPE_BASE ** (-jnp.arange(0, half, 2, dtype=F32) / half)
    ar = row[:, None] * inv
    ac = col[:, None] * inv
    ang = jnp.concatenate([ar, ar, ac, ac], axis=-1)
    sign = jnp.where(jnp.arange(DA_QK) % 32 < 16, -1.0, 1.0).astype(F32)
    cos = jnp.concatenate([jnp.ones((nctx, DA_QK), F32), jnp.cos(ang)], axis=0)
    sin = jnp.concatenate([jnp.zeros((nctx, DA_QK), F32), jnp.sin(ang) * sign], axis=0)
    cos = jnp.tile(cos, (1, 2))
    sin = jnp.tile(sin, (1, 2))
    return cos, sin


def _split_weights(w_in_l, b_gate_l):
    nq = DA_HEADS * 2 * DA_QK
    o = 0
    wdq = w_in_l[:, o:o + nq]; o += nq
    wdk = w_in_l[:, o:o + nq]; o += nq
    wdv = w_in_l[:, o:o + DA_HEADS * DA_V]; o += DA_HEADS * DA_V
    wmq = w_in_l[:, o:o + ML_HEADS * ML_QK]; o += ML_HEADS * ML_QK
    wmk = w_in_l[:, o:o + ML_HEADS * ML_QK]; o += ML_HEADS * ML_QK
    wmv = w_in_l[:, o:o + ML_HEADS * ML_V]; o += ML_HEADS * ML_V
    wmo = w_in_l[:, o:o + ML_HEADS * ML_V]; o += ML_HEADS * ML_V
    wmg = w_in_l[:, o:o + 4 * ML_HEADS]
    perm = jnp.array([0, 1, 2, 3, 8, 9, 10, 11, 4, 5, 6, 7, 12, 13, 14, 15])
    wmg = wmg[:, perm]
    bg = b_gate_l[perm].reshape(16, 1).astype(F32)
    qscale = DA_QK ** -0.5 * LOG2_E
    kscale = ML_QK ** -0.5
    wn = jnp.concatenate([wdk, wmk * kscale, wmo], axis=1).astype(BF16)
    wt = jnp.concatenate([wdq * qscale, wdv, wmq, wmv, wmg], axis=1).T.astype(BF16)
    return wn, wt, bg


def kernel(x, c, ctx, c_ctx, w_ada, b_ada, norm1_g, norm2_g, w_in, b_gate, lam_q1, lam_k1, lam_q2, lam_k2,
           subln_g, mlstm_norm_g, w_out, w_fc1, w_fc2, final_g):
    bsz, t, d = x.shape
    nctx = ctx.shape[1]
    depth = w_ada.shape[0]
    assert depth == 1, "single-layer block: the context stream is never updated"
    assert TOK_TILE == ML_CHUNK_LEN, "the projection computes per-chunk gate sums on its own tile"
    assert nctx % TOK_TILE == 0 and ATT_TQ % TOK_TILE == 0 and ATT_TK % TOK_TILE == 0
    assert t % ATT_TQ == 0 and t % ML_CHUNK_LEN == 0 and (nctx + t) % ATT_TK == 0 and t % GRID_W == 0
    assert t % FINAL_TILE == 0
    assert (t // ATT_TQ) * ((nctx + t) // ATT_TK) >= 3, "the attention pipeline needs at least three steps"
    lambda_init = 0.2

    cc = jnp.concatenate([c, c_ctx[None, :], jnp.zeros((8 - bsz - 1, d), F32)], axis=0)
    mod = _adaln(cc, w_ada[0], b_ada[0][None, :])
    mb = mod[:bsz].reshape(bsz, 1, 6, d)
    sh1, sc1, g1, sh2, sc2, g2 = [mb[:, :, k] for k in range(6)]
    mc = mod[bsz:bsz + 1].reshape(1, 1, 6, d)
    csh1, csc1 = mc[:, :, 0], mc[:, :, 1]

    wn, wt, bg = _split_weights(w_in[0], b_gate[0])
    cosn, sinn = _rope_tables(t, nctx)
    ka, km, mo, qat, vat, qmt, vmt, gt, cc = _project(
        x, ctx, sc1, sh1, csc1, csh1, norm1_g[0][None, :], wn, wt, bg, cosn, sinn)

    lamv = jnp.stack([lam_q1[0], lam_k1[0], lam_q2[0], lam_k2[0]]).astype(F32)
    da = _attention(qat, ka, vat, lamv, subln_g[0].reshape(DA_V, 1), t, lambda_init)
    hf, hb = _mlstm(qmt, km, vmt, gt, cc, t)

    return _final(x, da, hf, hb, mo, g1, sh2, sc2, g2, mlstm_norm_g[0][None, :], norm2_g[0][None, :],
                  final_g[None, :], w_out[0].astype(BF16), w_fc1[0].astype(BF16), w_fc2[0].astype(BF16))
```

```python
import functools

import jax
import jax.numpy as jnp
from jax import lax
from jax.experimental import pallas as pl
from jax.experimental.pallas import tpu as pltpu

DA_HEADS = 4
DA_QK = 64
DA_V = 128
ML_HEADS = 4
ML_QK = 64
ML_V = 128
GRID_W = 64
ROPE_BASE = 10000.0
EPS = 1e-6
LOG2_E = 1.4426950408889634

TOK_TILE = 256
FINAL_TILE = 512
ATT_TQ = 512
ATT_TK = 768
ATT_MAX_UNROLL = 2
ML_CHUNK_LEN = 256
ML_STATE_ROWS = 144
VMEM_LIMIT_BYTES = 56 * 1024 * 1024

F32 = jnp.float32
BF16 = jnp.bfloat16


def _rms(x, g):
    return x * lax.rsqrt(jnp.mean(x * x, axis=-1, keepdims=True) + EPS) * g


def _adaln_kernel(c_ref, w_ref, b_ref, o_ref):
    c = c_ref[...]
    s = (c * jax.nn.sigmoid(c)).astype(BF16)
    o_ref[...] = jnp.dot(s, w_ref[...].astype(BF16), preferred_element_type=F32) + b_ref[...]


def _adaln(cc, w, b):
    rows, d = cc.shape
    n = w.shape[1]
    bn = d
    return pl.pallas_call(
        _adaln_kernel,
        grid=(n // bn,),
        in_specs=[pl.BlockSpec((rows, d), lambda j: (0, 0)),
                  pl.BlockSpec((d, bn), lambda j: (0, j)),
                  pl.BlockSpec((1, bn), lambda j: (0, j))],
        out_specs=pl.BlockSpec((rows, bn), lambda j: (0, j)),
        out_shape=jax.ShapeDtypeStruct((rows, n), F32),
        name="adaln",
    )(cc, w, b)


def _proj_kernel(x_ref, ctx_ref, sc_ref, sh_ref, csc_ref, csh_ref, g_ref, wn_ref, wt_ref, bg_ref,
                 cosn_ref, sinn_ref,
                 ka_ref, km_ref, mo_ref, qat_ref, vat_ref, qmt_ref, vmt_ref, gt_ref, cc_ref, *, nct):
    i = pl.program_id(0)
    is_ctx = i < nct
    x = jnp.where(is_ctx, ctx_ref[0], x_ref[0])
    sc = jnp.where(is_ctx, csc_ref[0], sc_ref[0])
    sh = jnp.where(is_ctx, csh_ref[0], sh_ref[0])
    xm = _rms(x, g_ref[...]) * (1.0 + sc) + sh
    xb = xm.astype(BF16)
    nt = (((1,), (1,)), ((), ()))
    hd = DA_HEADS * 2 * DA_QK
    ngate = 2 * hd + 768

    g = lax.dot_general(wt_ref[ngate:ngate + 16, :], xb, nt, preferred_element_type=F32) + bg_ref[...]
    li8 = g[0:8]
    lf8 = jnp.minimum(g[8:16], 0.0) - jnp.log1p(jnp.exp(-jnp.abs(g[8:16])))
    tm = g.shape[1]
    row8 = lax.broadcasted_iota(jnp.int32, (8, tm), 0)
    lane8 = lax.broadcasted_iota(jnp.int32, (8, tm), 1)
    is_f = row8 < ML_HEADS

    def scan(x, op, fill):
        step = 1
        while step < tm:
            prev = jnp.where(lane8 >= step, pltpu.roll(x, step, axis=1), fill)
            nxt = jnp.where(lane8 < tm - step, pltpu.roll(x, tm - step, axis=1), fill)
            x = op(x, jnp.where(is_f, prev, nxt))
            step *= 2
        return x

    b8 = scan(lf8, jnp.add, 0.0)
    c8 = li8 - b8
    cm8 = scan(c8, jnp.maximum, -jnp.inf)
    gt_ref[0] = jnp.concatenate([b8, c8, cm8], axis=0) * LOG2_E
    cc_ref[0] = (jnp.concatenate([c8, jnp.zeros((120, tm), F32)], axis=0) * LOG2_E).T

    hn = jnp.dot(xb, wn_ref[...], preferred_element_type=F32)
    ht = lax.dot_general(wt_ref[0:ngate, :], xb, nt, preferred_element_type=F32)

    cosn, sinn = cosn_ref[...], sinn_ref[...]
    cost, sint = cosn.T, sinn.T
    lane = lax.broadcasted_iota(jnp.int32, (tm, 128), 1)
    low_half = (lane % 32) < 16
    for h in range(DA_HEADS):
        lo, hi = h * 128, (h + 1) * 128
        k = hn[:, lo:hi]
        kswap = jnp.where(low_half, pltpu.roll(k, 128 - 16, axis=1), pltpu.roll(k, 16, axis=1))
        ka_ref[0, :, lo:hi] = (k * cosn + kswap * sinn).astype(BF16)
        q = ht[lo:hi]
        qswap = jnp.concatenate([q[r0 + 16:r0 + 32] if part == 0 else q[r0:r0 + 16]
                                 for r0 in range(0, 128, 32) for part in (0, 1)], axis=0)
        q = (q * cost + qswap * sint).astype(BF16)
        qrow = lax.broadcasted_iota(jnp.int32, q.shape, 0)
        zero = jnp.zeros_like(q)
        qat_ref[0, h, 0] = jnp.concatenate(
            [jnp.where(qrow < DA_QK, q, zero), jnp.where(qrow >= DA_QK, q, zero)], axis=1)
    km_ref[0] = hn[:, hd:hd + 256].astype(BF16)
    mo_ref[0] = jax.nn.sigmoid(hn[:, hd + 256:hd + 768]).astype(BF16)

    vat_ref[0, 0] = ht[hd:2 * hd].astype(BF16)
    qmt_ref[0] = ht[2 * hd:2 * hd + 256].astype(BF16)
    vmt_ref[0] = ht[2 * hd + 256:2 * hd + 768].astype(BF16)


def _project(x, ctx, sc, sh, csc, csh, g, wn, wt, bg, cosn, sinn):
    bsz, t, d = x.shape
    nctx = ctx.shape[1]
    n = nctx + t
    tm = TOK_TILE
    nct = nctx // tm
    r = ATT_TK // tm
    hd = DA_HEADS * 2 * DA_QK
    const2 = lambda i, b: (0, 0)
    const3 = lambda i, b: (0, 0, 0)
    in_specs = [
        pl.BlockSpec((1, tm, d), lambda i, b: (b, jnp.maximum(i - nct, 0), 0)),
        pl.BlockSpec((1, tm, d), lambda i, b: (b, jnp.minimum(i, nct - 1), 0)),
        pl.BlockSpec((1, 1, d), lambda i, b: (b, 0, 0)),
        pl.BlockSpec((1, 1, d), lambda i, b: (b, 0, 0)),
        pl.BlockSpec((1, 1, d), const3),
        pl.BlockSpec((1, 1, d), const3),
        pl.BlockSpec((1, d), const2),
        pl.BlockSpec(wn.shape, const2),
        pl.BlockSpec(wt.shape, const2),
        pl.BlockSpec(bg.shape, const2),
        pl.BlockSpec((tm, 128), lambda i, b: (i, 0)),
        pl.BlockSpec((tm, 128), lambda i, b: (i, 0)),
    ]
    out_specs = [
        pl.BlockSpec((1, tm, hd), lambda i, b: (b, i, 0)),
        pl.BlockSpec((1, tm, 256), lambda i, b: (b, i, 0)),
        pl.BlockSpec((1, tm, 512), lambda i, b: (b, jnp.where(i < nct, n // tm - nct + i, i - nct), 0)),
        pl.BlockSpec((1, DA_HEADS, 1, 128, 2 * tm), lambda i, b: (b, 0, i, 0, 0)),
        pl.BlockSpec((1, 1, 512, tm), lambda i, b: (b, i // r, 0, i % r)),
        pl.BlockSpec((1, 256, tm), lambda i, b: (b, 0, i)),
        pl.BlockSpec((1, 512, tm), lambda i, b: (b, 0, i)),
        pl.BlockSpec((1, 24, tm), lambda i, b: (b, 0, i)),
        pl.BlockSpec((1, tm, 128), lambda i, b: (b, i, 0)),
    ]
    out_shape = [
        jax.ShapeDtypeStruct((bsz, n, hd), BF16),
        jax.ShapeDtypeStruct((bsz, n, 256), BF16),
        jax.ShapeDtypeStruct((bsz, n, 512), BF16),
        jax.ShapeDtypeStruct((bsz, DA_HEADS, n // tm, 128, 2 * tm), BF16),
        jax.ShapeDtypeStruct((bsz, n // ATT_TK, 512, ATT_TK), BF16),
        jax.ShapeDtypeStruct((bsz, 256, n), BF16),
        jax.ShapeDtypeStruct((bsz, 512, n), BF16),
        jax.ShapeDtypeStruct((bsz, 24, n), F32),
        jax.ShapeDtypeStruct((bsz, n, 128), F32),
    ]
    return pl.pallas_call(
        functools.partial(_proj_kernel, nct=nct),
        grid=(n // tm, bsz),
        in_specs=in_specs,
        out_specs=out_specs,
        out_shape=out_shape,
        compiler_params=pltpu.CompilerParams(
            dimension_semantics=("arbitrary", "arbitrary"), vmem_limit_bytes=VMEM_LIMIT_BYTES),
        name="project",
    )(x, ctx, sc, sh, csc, csh, g, wn, wt, bg, cosn, sinn)


def _attn_kernel(q_ref, k_ref, v_ref, lam_ref, g_ref, o_ref, m_s, l_s, lfin_s, acc_s, s_s, mb_s, al_s, p_s, *,
                 tq, tk, nblk, ntile, qsub, qoff, unroll, lambda_init):
    nstep = ntile * nblk
    sw = q_ref.shape[-1]
    half = sw // 2

    m_s[...] = jnp.zeros(m_s.shape, F32)
    l_s[...] = jnp.zeros(l_s.shape, F32)
    lfin_s[...] = jnp.ones(lfin_s.shape, F32)
    acc_s[...] = jnp.zeros(acc_s.shape, F32)

    lv = lam_ref[...]
    lam = (jnp.exp(jnp.sum(lv[0:1] * lv[1:2], axis=1, keepdims=True))
           - jnp.exp(jnp.sum(lv[2:3] * lv[3:4], axis=1, keepdims=True)) + lambda_init)

    def values(u):
        kb = lax.rem(u, nblk)
        slot = lax.rem(lax.div(u, nblk), 2)
        pv = jnp.dot(v_ref[0, kb], p_s[...], preferred_element_type=F32)
        acc_s[slot] = al_s[...] * acc_s[slot] + pv

    def scale(u):
        kb = lax.rem(u, nblk)
        first = kb == 0
        m_old = jnp.where(first, -jnp.inf, m_s[...])
        l_old = jnp.where(first, 0.0, l_s[...])
        m_new = jnp.maximum(m_old, mb_s[...])
        alpha = jnp.exp2(m_old - m_new)
        p = jnp.exp2(s_s[...] - m_new)
        l_new = alpha * l_old + jnp.sum(p, axis=0, keepdims=True)
        l_s[...] = l_new
        lfin_s[...] = jnp.where(kb == nblk - 1, l_new, lfin_s[...])
        p_s[...] = p.astype(BF16)
        al_s[...] = alpha
        m_s[...] = m_new

    def scores(u):
        tile = lax.div(u, nblk)
        kb = lax.rem(u, nblk)
        qbd = jnp.concatenate([q_ref[0, 0, qoff + tile * qsub + c] for c in range(qsub)], axis=1)
        k = k_ref[0, pl.ds(pl.multiple_of(kb * tk, tk), tk), :]
        s = jnp.dot(k, qbd, preferred_element_type=F32)
        s_s[...] = s
        mb_s[...] = jnp.max(s, axis=0, keepdims=True)

    def finalize(tile):
        inv_l = 1.0 / lfin_s[...]
        acc = acc_s[lax.rem(tile, 2)]
        for c in range(qsub):
            a1, a2 = c * sw, c * sw + half
            o = acc[:, a1:a2] * inv_l[:, a1:a2] - lam * (acc[:, a2:a2 + half] * inv_l[:, a2:a2 + half])
            y = o * lax.rsqrt(jnp.mean(o * o, axis=0, keepdims=True) + EPS) * g_ref[...] * (1.0 - lambda_init)
            o_ref[0, pl.ds(pl.multiple_of(tile * tq + c * half, half), half), :] = y.T.astype(BF16)

    scores(0)
    scale(0)
    scores(1)

    def body(i, carry):
        u = unroll * i + 1
        for d in range(unroll):
            values(u - 1 + d)
            scale(u + d)
            scores(u + 1 + d)
        for d in range(unroll):
            w = u - 1 + d

            @pl.when(lax.rem(w, nblk) == nblk - 1)
            def _():
                finalize(lax.div(w, nblk))
        return carry

    assert (nstep - 2) % unroll == 0 and nblk >= unroll
    lax.fori_loop(0, (nstep - 2) // unroll, body, 0)
    values(nstep - 2)
    scale(nstep - 1)
    values(nstep - 1)
    finalize(ntile - 1)


def _attn_unroll(loop_steps, nblk):
    return max(u for u in range(1, min(ATT_MAX_UNROLL, nblk) + 1) if loop_steps % u == 0)


def _attention(qat, ka, vat, lamv, subg, t, lambda_init):
    bsz, _, nsub, _, sw = qat.shape
    n = ka.shape[1]
    tq, tk = ATT_TQ, ATT_TK
    sub = sw // 2
    nblk = n // tk
    kern = functools.partial(_attn_kernel, tq=tq, tk=tk, nblk=nblk, ntile=t // tq, qsub=tq // sub,
                             qoff=(n - t) // sub, unroll=_attn_unroll((t // tq) * nblk - 2, nblk), lambda_init=lambda_init)
    row = lambda: pltpu.VMEM((1, 2 * tq), F32)
    return pl.pallas_call(
        kern,
        grid=(bsz, DA_HEADS),
        in_specs=[
            pl.BlockSpec((1, 1, nsub, 128, sw), lambda b, h: (b, h, 0, 0, 0)),
            pl.BlockSpec((1, n, 128), lambda b, h: (b, 0, h)),
            pl.BlockSpec((1, nblk, 128, tk), lambda b, h: (b, 0, h, 0)),
            pl.BlockSpec(lamv.shape, lambda b, h: (0, 0)),
            pl.BlockSpec(subg.shape, lambda b, h: (0, 0)),
        ],
        out_specs=pl.BlockSpec((1, t, 128), lambda b, h: (b, 0, h)),
        out_shape=jax.ShapeDtypeStruct((bsz, t, DA_HEADS * DA_V), BF16),
        scratch_shapes=[row(), row(), row(), pltpu.VMEM((2, DA_V, 2 * tq), F32), pltpu.VMEM((tk, 2 * tq), F32),
                        row(), row(), pltpu.VMEM((tk, 2 * tq), BF16)],
        compiler_params=pltpu.CompilerParams(
            dimension_semantics=("arbitrary", "arbitrary"), vmem_limit_bytes=VMEM_LIMIT_BYTES),
        name="diff_attn",
    )(qat, ka, vat, lamv, subg)


def _mlstm_kernel(qf_ref, kf_ref, vf_ref, gf_ref, cf_ref, qb_ref, kb_ref, vb_ref, gb_ref, cb_ref,
                  hf_ref, hb_ref, c_s, m_s, *, L):
    j = pl.program_id(1)

    @pl.when(j == 0)
    def _():
        c_s[...] = jnp.zeros(c_s.shape, F32)
        m_s[...] = jnp.zeros(m_s.shape, F32)

    sidx = lax.broadcasted_iota(jnp.int32, (L, L), 0)
    tidx = lax.broadcasted_iota(jnp.int32, (L, L), 1)
    row128 = lax.broadcasted_iota(jnp.int32, (128, L), 0)
    lane128 = lax.broadcasted_iota(jnp.int32, (ML_STATE_ROWS, 128), 1)
    row16 = lax.broadcasted_iota(jnp.int32, (16, L), 0)

    chains = range(2 * ML_HEADS)
    rev = [r >= ML_HEADS for r in chains]
    head = [r % ML_HEADS for r in chains]
    refs = [(qb_ref, kb_ref, vb_ref, gb_ref, cb_ref) if rev[r] else (qf_ref, kf_ref, vf_ref, gf_ref, cf_ref)
            for r in chains]
    pairs = [slice((head[r] // 2) * 128, (head[r] // 2 + 1) * 128) for r in chains]
    odd = [head[r] % 2 == 1 for r in chains]

    qpair = [refs[r][0][0, pairs[r], :] for r in chains]
    kpair = [refs[r][1][0, :, pairs[r]] for r in chains]
    vt = [refs[r][2][0, head[r] * ML_V:(head[r] + 1) * ML_V, :] for r in chains]
    b_row = [refs[r][3][0, r:r + 1, :] for r in chains]
    c_row = [refs[r][3][0, 8 + r:9 + r, :] for r in chains]
    cm_row = [refs[r][3][0, 16 + r:17 + r, :] for r in chains]
    c_col = [refs[r][4][0, :, r:r + 1] for r in chains]
    m_prev = [m_s[r] for r in chains]
    state = [c_s[r] for r in chains]
    m_run = [jnp.maximum(m_prev[r], cm_row[r]) for r in chains]

    qt = [jnp.where((row128 >= ML_QK) if odd[r] else (row128 < ML_QK), qpair[r], jnp.zeros_like(qpair[r]))
          for r in chains]
    st = [jnp.dot(kpair[r], qt[r], preferred_element_type=F32) for r in chains]
    cq = [jnp.dot(state[r].astype(BF16), qpair[r], preferred_element_type=F32) for r in chains]
    sp = [jnp.where((sidx >= tidx) if rev[r] else (sidx <= tidx),
                    st[r] * jnp.exp2(c_col[r] - m_run[r]), 0.0) for r in chains]
    den_intra = [jnp.sum(sp[r], axis=0, keepdims=True) for r in chains]
    num_intra = [jnp.dot(vt[r], sp[r].astype(BF16), preferred_element_type=F32) for r in chains]

    b_tot = [b_row[r][:, 0:1] if rev[r] else b_row[r][:, L - 1:L] for r in chains]
    cm_end = [cm_row[r][:, 0:1] if rev[r] else cm_row[r][:, L - 1:L] for r in chains]
    m_end = [jnp.maximum(m_prev[r][:, 0:1], cm_end[r]) for r in chains]
    w_row = [jnp.exp2(c_row[r] - m_end[r]) for r in chains]
    aug = [jnp.concatenate([vt[r].astype(F32) * w_row[r],
                            jnp.where(row16 == 0, jnp.broadcast_to(w_row[r], (16, L)), 0.0)], axis=0)
           for r in chains]
    upd = [jnp.dot(aug[r].astype(BF16), kpair[r], preferred_element_type=F32) for r in chains]

    for r in chains:
        w_inter = jnp.exp2(m_prev[r] - m_run[r])
        num = num_intra[r] + w_inter * cq[r][0:ML_V]
        den = den_intra[r] + w_inter * cq[r][ML_V:ML_V + 1]
        ht = num * (1.0 / jnp.maximum(jnp.abs(den), jnp.exp2(-(b_row[r] + m_run[r]))))
        out_ref = hb_ref if rev[r] else hf_ref
        out_ref[0, :, head[r] * ML_V:(head[r] + 1) * ML_V] = ht.T
        own_lanes = (lane128 >= ML_QK) if odd[r] else (lane128 < ML_QK)
        c_s[r] = jnp.exp2(m_prev[r][:, 0:1] - m_end[r]) * state[r] + jnp.where(own_lanes, upd[r], 0.0)
        m_s[r] = jnp.broadcast_to(b_tot[r] + m_end[r], (1, L))


def _mlstm(qmt, km, vmt, gt, cc, t):
    bsz, _, n = qmt.shape
    L = ML_CHUNK_LEN
    nch = n // L
    nctx_ch = (n - t) // L
    nlat = t // L

    def fwd_c(j):
        return j

    def bwd_c(j):
        return jnp.where(j < nctx_ch, nctx_ch - 1 - j, nch - 1 - (j - nctx_ch))

    def specs(cfn):
        return [
            pl.BlockSpec((1, 256, L), lambda b, j: (b, 0, cfn(j))),
            pl.BlockSpec((1, L, 256), lambda b, j: (b, cfn(j), 0)),
            pl.BlockSpec((1, 512, L), lambda b, j: (b, 0, cfn(j))),
            pl.BlockSpec((1, 24, L), lambda b, j: (b, 0, cfn(j))),
            pl.BlockSpec((1, L, 128), lambda b, j: (b, cfn(j), 0)),
        ]

    lat = lambda j: jnp.maximum(j - nctx_ch, 0)
    out_specs = [
        pl.BlockSpec((1, L, 512), lambda b, j: (b, lat(j), 0)),
        pl.BlockSpec((1, L, 512), lambda b, j: (b, nlat - 1 - lat(j), 0)),
    ]
    out_shape = [jax.ShapeDtypeStruct((bsz, t, 512), F32)] * 2
    return pl.pallas_call(
        functools.partial(_mlstm_kernel, L=L),
        grid=(bsz, nch),
        in_specs=specs(fwd_c) + specs(bwd_c),
        out_specs=out_specs,
        out_shape=out_shape,
        scratch_shapes=[pltpu.VMEM((2 * ML_HEADS, ML_STATE_ROWS, 128), F32),
                        pltpu.VMEM((2 * ML_HEADS, 1, L), F32)],
        compiler_params=pltpu.CompilerParams(
            dimension_semantics=("arbitrary", "arbitrary"), vmem_limit_bytes=VMEM_LIMIT_BYTES),
        name="mlstm",
    )(qmt, km, vmt, gt, cc, qmt, km, vmt, gt, cc)


def _final_kernel(x_ref, da_ref, hf_ref, hb_ref, mo_ref, g1_ref, sh2_ref, sc2_ref, g2_ref,
                  mlg_ref, n2g_ref, fg_ref, wo_ref, w1_ref, w2_ref, o_ref):
    x = x_ref[0]
    hsum = hf_ref[0] + hb_ref[0]
    parts = []
    for h in range(ML_HEADS):
        u = hsum[:, h * ML_V:(h + 1) * ML_V]
        parts.append(u * lax.rsqrt(jnp.mean(u * u, axis=1, keepdims=True) + EPS))
    ml = jnp.concatenate(parts, axis=1) * mlg_ref[...] * mo_ref[0].astype(F32)
    cat = jnp.concatenate([da_ref[0], ml.astype(BF16)], axis=1)
    y = jnp.dot(cat, wo_ref[...], preferred_element_type=F32)
    x1 = x + g1_ref[0] * y
    xn = (_rms(x1, n2g_ref[...]) * (1.0 + sc2_ref[0]) + sh2_ref[0]).astype(BF16)
    hid = jnp.dot(xn, w1_ref[...], preferred_element_type=F32)
    hid = jnp.square(jnp.maximum(hid, 0.0)).astype(BF16)
    x2 = x1 + g2_ref[0] * jnp.dot(hid, w2_ref[...], preferred_element_type=F32)
    o_ref[0] = _rms(x2, fg_ref[...])


def _final(x, da, hf, hb, mo, g1, sh2, sc2, g2, mlg, n2g, fg, wo, w1, w2):
    bsz, t, d = x.shape
    tm = FINAL_TILE
    tok = lambda w: pl.BlockSpec((1, tm, w), lambda b, i: (b, i, 0))
    mod = pl.BlockSpec((1, 1, d), lambda b, i: (b, 0, 0))
    const = lambda a: pl.BlockSpec(a.shape, lambda b, i: (0, 0), pipeline_mode=pl.Buffered(1))
    return pl.pallas_call(
        _final_kernel,
        grid=(bsz, t // tm),
        in_specs=[tok(d), tok(512), tok(512), tok(512), tok(512),
                  mod, mod, mod, mod, const(mlg), const(n2g), const(fg), const(wo), const(w1), const(w2)],
        out_specs=tok(d),
        out_shape=jax.ShapeDtypeStruct((bsz, t, d), F32),
        compiler_params=pltpu.CompilerParams(
            dimension_semantics=("arbitrary", "arbitrary"), vmem_limit_bytes=VMEM_LIMIT_BYTES),
        name="final",
    )(x, da, hf, hb, mo, g1, sh2, sc2, g2, mlg, n2g, fg, wo, w1, w2)


def _rope_tables(t, nctx):
    rows = t // GRID_W
    row = jnp.repeat(jnp.arange(rows, dtype=F32), GRID_W)
    col = jnp.tile(jnp.arange(GRID_W, dtype=F32), rows)
    half = DA_QK // 2
    inv = ROPE_BASE ** (-jnp.arange(0, half, 2, dtype=F32) / half)
    ar = row[:, None] * inv
    ac = col[:, None] * inv
    ang = jnp.concatenate([ar, ar, ac, ac], axis=-1)
    sign = jnp.where(jnp.arange(DA_QK) % 32 < 16, -1.0, 1.0).astype(F32)
    cos = jnp.concatenate([jnp.ones((nctx, DA_QK), F32), jnp.cos(ang)], axis=0)
    sin = jnp.concatenate([jnp.zeros((nctx, DA_QK), F32), jnp.sin(ang) * sign], axis=0)
    cos = jnp.tile(cos, (1, 2))
    sin = jnp.tile(sin, (1, 2))
    return cos, sin


def _split_weights(w_in_l, b_gate_l):
    nq = DA_HEADS * 2 * DA_QK
    o = 0
    wdq = w_in_l[:, o:o + nq]; o += nq
    wdk = w_in_l[:, o:o + nq]; o += nq
    wdv = w_in_l[:, o:o + DA_HEADS * DA_V]; o += DA_HEADS * DA_V
    wmq = w_in_l[:, o:o + ML_HEADS * ML_QK]; o += ML_HEADS * ML_QK
    wmk = w_in_l[:, o:o + ML_HEADS * ML_QK]; o += ML_HEADS * ML_QK
    wmv = w_in_l[:, o:o + ML_HEADS * ML_V]; o += ML_HEADS * ML_V
    wmo = w_in_l[:, o:o + ML_HEADS * ML_V]; o += ML_HEADS * ML_V
    wmg = w_in_l[:, o:o + 4 * ML_HEADS]
    perm = jnp.array([0, 1, 2, 3, 8, 9, 10, 11, 4, 5, 6, 7, 12, 13, 14, 15])
    wmg = wmg[:, perm]
    bg = b_gate_l[perm].reshape(16, 1).astype(F32)
    qscale = DA_QK ** -0.5 * LOG2_E
    kscale = ML_QK ** -0.5
    wn = jnp.concatenate([wdk, wmk * kscale, wmo], axis=1).astype(BF16)
    wt = jnp.concatenate([wdq * qscale, wdv, wmq, wmv, wmg], axis=1).T.astype(BF16)
    return wn, wt, bg


def kernel(x, c, ctx, c_ctx, w_ada, b_ada, norm1_g, norm2_g, w_in, b_gate, lam_q1, lam_k1, lam_q2, lam_k2,
           subln_g, mlstm_norm_g, w_out, w_fc1, w_fc2, final_g):
    bsz, t, d = x.shape
    nctx = ctx.shape[1]
    depth = w_ada.shape[0]
    assert depth == 1, "single-layer block: the context stream is never updated"
    assert TOK_TILE == ML_CHUNK_LEN, "the projection computes per-chunk gate sums on its own tile"
    assert nctx % TOK_TILE == 0 and ATT_TQ % TOK_TILE == 0 and ATT_TK % TOK_TILE == 0
    assert t % ATT_TQ == 0 and t % ML_CHUNK_LEN == 0 and (nctx + t) % ATT_TK == 0 and t % GRID_W == 0
    assert t % FINAL_TILE == 0
    assert (t // ATT_TQ) * ((nctx + t) // ATT_TK) >= 3, "the attention pipeline needs at least three steps"
    lambda_init = 0.2

    cc = jnp.concatenate([c, c_ctx[None, :], jnp.zeros((8 - bsz - 1, d), F32)], axis=0)
    mod = _adaln(cc, w_ada[0], b_ada[0][None, :])
    mb = mod[:bsz].reshape(bsz, 1, 6, d)
    sh1, sc1, g1, sh2, sc2, g2 = [mb[:, :, k] for k in range(6)]
    mc = mod[bsz:bsz + 1].reshape(1, 1, 6, d)
    csh1, csc1 = mc[:, :, 0], mc[:, :, 1]

    wn, wt, bg = _split_weights(w_in[0], b_gate[0])
    cosn, sinn = _rope_tables(t, nctx)
    ka, km, mo, qat, vat, qmt, vmt, gt, cc = _project(
        x, ctx, sc1, sh1, csc1, csh1, norm1_g[0][None, :], wn, wt, bg, cosn, sinn)

    lamv = jnp.stack([lam_q1[0], lam_k1[0], lam_q2[0], lam_k2[0]]).astype(F32)
    da = _attention(qat, ka, vat, lamv, subln_g[0].reshape(DA_V, 1), t, lambda_init)
    hf, hb = _mlstm(qmt, km, vmt, gt, cc, t)

    return _final(x, da, hf, hb, mo, g1, sh2, sc2, g2, mlstm_norm_g[0][None, :], norm2_g[0][None, :],
                  final_g[None, :], w_out[0].astype(BF16), w_fc1[0].astype(BF16), w_fc2[0].astype(BF16))
```

```python
import functools

import jax
import jax.numpy as jnp
from jax import lax
from jax.experimental import pallas as pl
from jax.experimental.pallas import tpu as pltpu

DA_HEADS = 4
DA_QK = 64
DA_V = 128
ML_HEADS = 4
ML_QK = 64
ML_V = 128
GRID_W = 64
ROPE_BASE = 10000.0
EPS = 1e-6
LOG2_E = 1.4426950408889634

TOK_TILE = 256
FINAL_TILE = 512
ATT_TQ = 512
ATT_TK = 768
ATT_MAX_UNROLL = 2
SCORE_PAD_LANES = 256
ML_CHUNK_LEN = 256
ML_STATE_ROWS = 144
VMEM_LIMIT_BYTES = 56 * 1024 * 1024

F32 = jnp.float32
BF16 = jnp.bfloat16


def _rms(x, g):
    return x * lax.rsqrt(jnp.mean(x * x, axis=-1, keepdims=True) + EPS) * g


def _adaln_kernel(c_ref, w_ref, b_ref, o_ref):
    c = c_ref[...]
    s = (c * jax.nn.sigmoid(c)).astype(BF16)
    o_ref[...] = jnp.dot(s, w_ref[...].astype(BF16), preferred_element_type=F32) + b_ref[...]


def _adaln(cc, w, b):
    rows, d = cc.shape
    n = w.shape[1]
    bn = d
    return pl.pallas_call(
        _adaln_kernel,
        grid=(n // bn,),
        in_specs=[pl.BlockSpec((rows, d), lambda j: (0, 0)),
                  pl.BlockSpec((d, bn), lambda j: (0, j)),
                  pl.BlockSpec((1, bn), lambda j: (0, j))],
        out_specs=pl.BlockSpec((rows, bn), lambda j: (0, j)),
        out_shape=jax.ShapeDtypeStruct((rows, n), F32),
        name="adaln",
    )(cc, w, b)


def _proj_kernel(x_ref, ctx_ref, sc_ref, sh_ref, csc_ref, csh_ref, g_ref, wn_ref, wt_ref, bg_ref,
                 cosn_ref, sinn_ref,
                 ka_ref, km_ref, mo_ref, qat_ref, vat_ref, qmt_ref, vmt_ref, gt_ref, cc_ref, *, nct):
    i = pl.program_id(0)
    is_ctx = i < nct
    x = jnp.where(is_ctx, ctx_ref[0], x_ref[0])
    sc = jnp.where(is_ctx, csc_ref[0], sc_ref[0])
    sh = jnp.where(is_ctx, csh_ref[0], sh_ref[0])
    xm = _rms(x, g_ref[...]) * (1.0 + sc) + sh
    xb = xm.astype(BF16)
    nt = (((1,), (1,)), ((), ()))
    hd = DA_HEADS * 2 * DA_QK
    ngate = 2 * hd + 768

    g = lax.dot_general(wt_ref[ngate:ngate + 16, :], xb, nt, preferred_element_type=F32) + bg_ref[...]
    li8 = g[0:8]
    lf8 = jnp.minimum(g[8:16], 0.0) - jnp.log1p(jnp.exp(-jnp.abs(g[8:16])))
    tm = g.shape[1]
    row8 = lax.broadcasted_iota(jnp.int32, (8, tm), 0)
    lane8 = lax.broadcasted_iota(jnp.int32, (8, tm), 1)
    is_f = row8 < ML_HEADS

    def scan(x, op, fill):
        step = 1
        while step < tm:
            prev = jnp.where(lane8 >= step, pltpu.roll(x, step, axis=1), fill)
            nxt = jnp.where(lane8 < tm - step, pltpu.roll(x, tm - step, axis=1), fill)
            x = op(x, jnp.where(is_f, prev, nxt))
            step *= 2
        return x

    b8 = scan(lf8, jnp.add, 0.0)
    c8 = li8 - b8
    cm8 = scan(c8, jnp.maximum, -jnp.inf)
    gt_ref[0] = jnp.concatenate([b8, c8, cm8], axis=0) * LOG2_E
    cc_ref[0] = (jnp.concatenate([c8, jnp.zeros((120, tm), F32)], axis=0) * LOG2_E).T

    hn = jnp.dot(xb, wn_ref[...], preferred_element_type=F32)
    ht = lax.dot_general(wt_ref[0:ngate, :], xb, nt, preferred_element_type=F32)

    cosn, sinn = cosn_ref[...], sinn_ref[...]
    cost, sint = cosn.T, sinn.T
    lane = lax.broadcasted_iota(jnp.int32, (tm, 128), 1)
    low_half = (lane % 32) < 16
    for h in range(DA_HEADS):
        lo, hi = h * 128, (h + 1) * 128
        k = hn[:, lo:hi]
        kswap = jnp.where(low_half, pltpu.roll(k, 128 - 16, axis=1), pltpu.roll(k, 16, axis=1))
        ka_ref[0, :, lo:hi] = (k * cosn + kswap * sinn).astype(BF16)
        q = ht[lo:hi]
        qswap = jnp.concatenate([q[r0 + 16:r0 + 32] if part == 0 else q[r0:r0 + 16]
                                 for r0 in range(0, 128, 32) for part in (0, 1)], axis=0)
        q = (q * cost + qswap * sint).astype(BF16)
        qrow = lax.broadcasted_iota(jnp.int32, q.shape, 0)
        zero = jnp.zeros_like(q)
        qat_ref[0, h, 0] = jnp.concatenate(
            [jnp.where(qrow < DA_QK, q, zero), jnp.where(qrow >= DA_QK, q, zero)], axis=1)
    km_ref[0] = hn[:, hd:hd + 256].astype(BF16)
    mo_ref[0] = jax.nn.sigmoid(hn[:, hd + 256:hd + 768]).astype(BF16)

    vat_ref[0, 0] = ht[hd:2 * hd].astype(BF16)
    qmt_ref[0] = ht[2 * hd:2 * hd + 256].astype(BF16)
    vmt_ref[0] = ht[2 * hd + 256:2 * hd + 768].astype(BF16)


def _project(x, ctx, sc, sh, csc, csh, g, wn, wt, bg, cosn, sinn):
    bsz, t, d = x.shape
    nctx = ctx.shape[1]
    n = nctx + t
    tm = TOK_TILE
    nct = nctx // tm
    r = ATT_TK // tm
    hd = DA_HEADS * 2 * DA_QK
    const2 = lambda i, b: (0, 0)
    const3 = lambda i, b: (0, 0, 0)
    in_specs = [
        pl.BlockSpec((1, tm, d), lambda i, b: (b, jnp.maximum(i - nct, 0), 0)),
        pl.BlockSpec((1, tm, d), lambda i, b: (b, jnp.minimum(i, nct - 1), 0)),
        pl.BlockSpec((1, 1, d), lambda i, b: (b, 0, 0)),
        pl.BlockSpec((1, 1, d), lambda i, b: (b, 0, 0)),
        pl.BlockSpec((1, 1, d), const3),
        pl.BlockSpec((1, 1, d), const3),
        pl.BlockSpec((1, d), const2),
        pl.BlockSpec(wn.shape, const2),
        pl.BlockSpec(wt.shape, const2),
        pl.BlockSpec(bg.shape, const2),
        pl.BlockSpec((tm, 128), lambda i, b: (i, 0)),
        pl.BlockSpec((tm, 128), lambda i, b: (i, 0)),
    ]
    out_specs = [
        pl.BlockSpec((1, tm, hd), lambda i, b: (b, i, 0)),
        pl.BlockSpec((1, tm, 256), lambda i, b: (b, i, 0)),
        pl.BlockSpec((1, tm, 512), lambda i, b: (b, jnp.where(i < nct, n // tm - nct + i, i - nct), 0)),
        pl.BlockSpec((1, DA_HEADS, 1, 128, 2 * tm), lambda i, b: (b, 0, i, 0, 0)),
        pl.BlockSpec((1, 1, 512, tm), lambda i, b: (b, i // r, 0, i % r)),
        pl.BlockSpec((1, 256, tm), lambda i, b: (b, 0, i)),
        pl.BlockSpec((1, 512, tm), lambda i, b: (b, 0, i)),
        pl.BlockSpec((1, 24, tm), lambda i, b: (b, 0, i)),
        pl.BlockSpec((1, tm, 128), lambda i, b: (b, i, 0)),
    ]
    out_shape = [
        jax.ShapeDtypeStruct((bsz, n, hd), BF16),
        jax.ShapeDtypeStruct((bsz, n, 256), BF16),
        jax.ShapeDtypeStruct((bsz, n, 512), BF16),
        jax.ShapeDtypeStruct((bsz, DA_HEADS, n // tm, 128, 2 * tm), BF16),
        jax.ShapeDtypeStruct((bsz, n // ATT_TK, 512, ATT_TK), BF16),
        jax.ShapeDtypeStruct((bsz, 256, n), BF16),
        jax.ShapeDtypeStruct((bsz, 512, n), BF16),
        jax.ShapeDtypeStruct((bsz, 24, n), F32),
        jax.ShapeDtypeStruct((bsz, n, 128), F32),
    ]
    return pl.pallas_call(
        functools.partial(_proj_kernel, nct=nct),
        grid=(n // tm, bsz),
        in_specs=in_specs,
        out_specs=out_specs,
        out_shape=out_shape,
        compiler_params=pltpu.CompilerParams(
            dimension_semantics=("arbitrary", "arbitrary"), vmem_limit_bytes=VMEM_LIMIT_BYTES),
        name="project",
    )(x, ctx, sc, sh, csc, csh, g, wn, wt, bg, cosn, sinn)


def _attn_kernel(q_ref, k_ref, v_ref, lam_ref, g_ref, o_ref, m_s, l_s, lfin_s, acc_s, s_s, mb_s, al_s, p_s, *,
                 tq, tk, nblk, ntile, qsub, qoff, unroll, lambda_init):
    nstep = ntile * nblk
    sw = q_ref.shape[-1]
    half = sw // 2

    m_s[...] = jnp.zeros(m_s.shape, F32)
    l_s[...] = jnp.zeros(l_s.shape, F32)
    lfin_s[...] = jnp.ones(lfin_s.shape, F32)
    acc_s[...] = jnp.zeros(acc_s.shape, F32)

    lv = lam_ref[...]
    lam = (jnp.exp(jnp.sum(lv[0:1] * lv[1:2], axis=1, keepdims=True))
           - jnp.exp(jnp.sum(lv[2:3] * lv[3:4], axis=1, keepdims=True)) + lambda_init)

    def values(u):
        kb = lax.rem(u, nblk)
        slot = lax.rem(lax.div(u, nblk), 2)
        pv = jnp.dot(v_ref[0, kb], p_s[...], preferred_element_type=F32)
        acc_s[slot] = al_s[...] * acc_s[slot] + pv

    def scale(u):
        kb = lax.rem(u, nblk)
        first = kb == 0
        m_old = jnp.where(first, -jnp.inf, m_s[...])
        l_old = jnp.where(first, 0.0, l_s[...])
        m_new = jnp.maximum(m_old, mb_s[...])
        alpha = jnp.exp2(m_old - m_new)
        p = jnp.exp2(s_s[:, 0:2 * tq] - m_new)
        l_new = alpha * l_old + jnp.sum(p, axis=0, keepdims=True)
        l_s[...] = l_new
        lfin_s[...] = jnp.where(kb == nblk - 1, l_new, lfin_s[...])
        p_s[...] = p.astype(BF16)
        al_s[...] = alpha
        m_s[...] = m_new

    def scores(u):
        tile = lax.div(u, nblk)
        kb = lax.rem(u, nblk)
        qbd = jnp.concatenate([q_ref[0, 0, qoff + tile * qsub + c] for c in range(qsub)], axis=1)
        k = k_ref[0, pl.ds(pl.multiple_of(kb * tk, tk), tk), :]
        s = jnp.dot(k, qbd, preferred_element_type=F32)
        s_s[:, 0:2 * tq] = s
        mb_s[...] = jnp.max(s, axis=0, keepdims=True)

    def finalize(tile):
        inv_l = 1.0 / lfin_s[...]
        acc = acc_s[lax.rem(tile, 2)]
        for c in range(qsub):
            a1, a2 = c * sw, c * sw + half
            o = acc[:, a1:a2] * inv_l[:, a1:a2] - lam * (acc[:, a2:a2 + half] * inv_l[:, a2:a2 + half])
            y = o * lax.rsqrt(jnp.mean(o * o, axis=0, keepdims=True) + EPS) * g_ref[...] * (1.0 - lambda_init)
            o_ref[0, pl.ds(pl.multiple_of(tile * tq + c * half, half), half), :] = y.T.astype(BF16)

    scores(0)
    scale(0)
    scores(1)

    def body(i, carry):
        u = unroll * i + 1
        for d in range(unroll):
            values(u - 1 + d)
            scale(u + d)
            scores(u + 1 + d)
        for d in range(unroll):
            w = u - 1 + d

            @pl.when(lax.rem(w, nblk) == nblk - 1)
            def _():
                finalize(lax.div(w, nblk))
        return carry

    assert (nstep - 2) % unroll == 0 and nblk >= unroll
    lax.fori_loop(0, (nstep - 2) // unroll, body, 0)
    values(nstep - 2)
    scale(nstep - 1)
    values(nstep - 1)
    finalize(ntile - 1)


def _attn_unroll(loop_steps, nblk):
    return max(u for u in range(1, min(ATT_MAX_UNROLL, nblk) + 1) if loop_steps % u == 0)


def _attention(qat, ka, vat, lamv, subg, t, lambda_init):
    bsz, _, nsub, _, sw = qat.shape
    n = ka.shape[1]
    tq, tk = ATT_TQ, ATT_TK
    sub = sw // 2
    nblk = n // tk
    kern = functools.partial(_attn_kernel, tq=tq, tk=tk, nblk=nblk, ntile=t // tq, qsub=tq // sub,
                             qoff=(n - t) // sub, unroll=_attn_unroll((t // tq) * nblk - 2, nblk), lambda_init=lambda_init)
    row = lambda: pltpu.VMEM((1, 2 * tq), F32)
    return pl.pallas_call(
        kern,
        grid=(bsz, DA_HEADS),
        in_specs=[
            pl.BlockSpec((1, 1, nsub, 128, sw), lambda b, h: (b, h, 0, 0, 0)),
            pl.BlockSpec((1, n, 128), lambda b, h: (b, 0, h)),
            pl.BlockSpec((1, nblk, 128, tk), lambda b, h: (b, 0, h, 0)),
            pl.BlockSpec(lamv.shape, lambda b, h: (0, 0)),
            pl.BlockSpec(subg.shape, lambda b, h: (0, 0)),
        ],
        out_specs=pl.BlockSpec((1, t, 128), lambda b, h: (b, 0, h)),
        out_shape=jax.ShapeDtypeStruct((bsz, t, DA_HEADS * DA_V), BF16),
        scratch_shapes=[row(), row(), row(), pltpu.VMEM((2, DA_V, 2 * tq), F32),
                        pltpu.VMEM((tk, 2 * tq + SCORE_PAD_LANES), F32),
                        row(), row(), pltpu.VMEM((tk, 2 * tq), BF16)],
        compiler_params=pltpu.CompilerParams(
            dimension_semantics=("arbitrary", "arbitrary"), vmem_limit_bytes=VMEM_LIMIT_BYTES),
        name="diff_attn",
    )(qat, ka, vat, lamv, subg)


def _mlstm_kernel(qf_ref, kf_ref, vf_ref, gf_ref, cf_ref, qb_ref, kb_ref, vb_ref, gb_ref, cb_ref,
                  hf_ref, hb_ref, c_s, m_s, *, L):
    j = pl.program_id(1)

    @pl.when(j == 0)
    def _():
        c_s[...] = jnp.zeros(c_s.shape, F32)
        m_s[...] = jnp.zeros(m_s.shape, F32)

    sidx = lax.broadcasted_iota(jnp.int32, (L, L), 0)
    tidx = lax.broadcasted_iota(jnp.int32, (L, L), 1)
    row128 = lax.broadcasted_iota(jnp.int32, (128, L), 0)
    lane128 = lax.broadcasted_iota(jnp.int32, (ML_STATE_ROWS, 128), 1)
    row16 = lax.broadcasted_iota(jnp.int32, (16, L), 0)

    chains = range(2 * ML_HEADS)
    rev = [r >= ML_HEADS for r in chains]
    head = [r % ML_HEADS for r in chains]
    refs = [(qb_ref, kb_ref, vb_ref, gb_ref, cb_ref) if rev[r] else (qf_ref, kf_ref, vf_ref, gf_ref, cf_ref)
            for r in chains]
    pairs = [slice((head[r] // 2) * 128, (head[r] // 2 + 1) * 128) for r in chains]
    odd = [head[r] % 2 == 1 for r in chains]

    qpair = [refs[r][0][0, pairs[r], :] for r in chains]
    kpair = [refs[r][1][0, :, pairs[r]] for r in chains]
    vt = [refs[r][2][0, head[r] * ML_V:(head[r] + 1) * ML_V, :] for r in chains]
    b_row = [refs[r][3][0, r:r + 1, :] for r in chains]
    c_row = [refs[r][3][0, 8 + r:9 + r, :] for r in chains]
    cm_row = [refs[r][3][0, 16 + r:17 + r, :] for r in chains]
    c_col = [refs[r][4][0, :, r:r + 1] for r in chains]
    m_prev = [m_s[r] for r in chains]
    state = [c_s[r] for r in chains]
    m_run = [jnp.maximum(m_prev[r], cm_row[r]) for r in chains]

    qt = [jnp.where((row128 >= ML_QK) if odd[r] else (row128 < ML_QK), qpair[r], jnp.zeros_like(qpair[r]))
          for r in chains]
    st = [jnp.dot(kpair[r], qt[r], preferred_element_type=F32) for r in chains]
    cq = [jnp.dot(state[r].astype(BF16), qpair[r], preferred_element_type=F32) for r in chains]
    sp = [jnp.where((sidx >= tidx) if rev[r] else (sidx <= tidx),
                    st[r] * jnp.exp2(c_col[r] - m_run[r]), 0.0) for r in chains]
    den_intra = [jnp.sum(sp[r], axis=0, keepdims=True) for r in chains]
    num_intra = [jnp.dot(vt[r], sp[r].astype(BF16), preferred_element_type=F32) for r in chains]

    b_tot = [b_row[r][:, 0:1] if rev[r] else b_row[r][:, L - 1:L] for r in chains]
    cm_end = [cm_row[r][:, 0:1] if rev[r] else cm_row[r][:, L - 1:L] for r in chains]
    m_end = [jnp.maximum(m_prev[r][:, 0:1], cm_end[r]) for r in chains]
    w_row = [jnp.exp2(c_row[r] - m_end[r]) for r in chains]
    aug = [jnp.concatenate([vt[r].astype(F32) * w_row[r],
                            jnp.where(row16 == 0, jnp.broadcast_to(w_row[r], (16, L)), 0.0)], axis=0)
           for r in chains]
    upd = [jnp.dot(aug[r].astype(BF16), kpair[r], preferred_element_type=F32) for r in chains]

    for r in chains:
        w_inter = jnp.exp2(m_prev[r] - m_run[r])
        num = num_intra[r] + w_inter * cq[r][0:ML_V]
        den = den_intra[r] + w_inter * cq[r][ML_V:ML_V + 1]
        ht = num * (1.0 / jnp.maximum(jnp.abs(den), jnp.exp2(-(b_row[r] + m_run[r]))))
        out_ref = hb_ref if rev[r] else hf_ref
        out_ref[0, :, head[r] * ML_V:(head[r] + 1) * ML_V] = ht.T
        own_lanes = (lane128 >= ML_QK) if odd[r] else (lane128 < ML_QK)
        c_s[r] = jnp.exp2(m_prev[r][:, 0:1] - m_end[r]) * state[r] + jnp.where(own_lanes, upd[r], 0.0)
        m_s[r] = jnp.broadcast_to(b_tot[r] + m_end[r], (1, L))


def _mlstm(qmt, km, vmt, gt, cc, t):
    bsz, _, n = qmt.shape
    L = ML_CHUNK_LEN
    nch = n // L
    nctx_ch = (n - t) // L
    nlat = t // L

    def fwd_c(j):
        return j

    def bwd_c(j):
        return jnp.where(j < nctx_ch, nctx_ch - 1 - j, nch - 1 - (j - nctx_ch))

    def specs(cfn):
        return [
            pl.BlockSpec((1, 256, L), lambda b, j: (b, 0, cfn(j))),
            pl.BlockSpec((1, L, 256), lambda b, j: (b, cfn(j), 0)),
            pl.BlockSpec((1, 512, L), lambda b, j: (b, 0, cfn(j))),
            pl.BlockSpec((1, 24, L), lambda b, j: (b, 0, cfn(j))),
            pl.BlockSpec((1, L, 128), lambda b, j: (b, cfn(j), 0)),
        ]

    lat = lambda j: jnp.maximum(j - nctx_ch, 0)
    out_specs = [
        pl.BlockSpec((1, L, 512), lambda b, j: (b, lat(j), 0)),
        pl.BlockSpec((1, L, 512), lambda b, j: (b, nlat - 1 - lat(j), 0)),
    ]
    out_shape = [jax.ShapeDtypeStruct((bsz, t, 512), F32)] * 2
    return pl.pallas_call(
        functools.partial(_mlstm_kernel, L=L),
        grid=(bsz, nch),
        in_specs=specs(fwd_c) + specs(bwd_c),
        out_specs=out_specs,
        out_shape=out_shape,
        scratch_shapes=[pltpu.VMEM((2 * ML_HEADS, ML_STATE_ROWS, 128), F32),
                        pltpu.VMEM((2 * ML_HEADS, 1, L), F32)],
        compiler_params=pltpu.CompilerParams(
            dimension_semantics=("arbitrary", "arbitrary"), vmem_limit_bytes=VMEM_LIMIT_BYTES),
        name="mlstm",
    )(qmt, km, vmt, gt, cc, qmt, km, vmt, gt, cc)


def _final_kernel(x_ref, da_ref, hf_ref, hb_ref, mo_ref, g1_ref, sh2_ref, sc2_ref, g2_ref,
                  mlg_ref, n2g_ref, fg_ref, wo_ref, w1_ref, w2_ref, o_ref):
    x = x_ref[0]
    hsum = hf_ref[0] + hb_ref[0]
    parts = []
    for h in range(ML_HEADS):
        u = hsum[:, h * ML_V:(h + 1) * ML_V]
        parts.append(u * lax.rsqrt(jnp.mean(u * u, axis=1, keepdims=True) + EPS))
    ml = jnp.concatenate(parts, axis=1) * mlg_ref[...] * mo_ref[0].astype(F32)
    cat = jnp.concatenate([da_ref[0], ml.astype(BF16)], axis=1)
    y = jnp.dot(cat, wo_ref[...], preferred_element_type=F32)
    x1 = x + g1_ref[0] * y
    xn = (_rms(x1, n2g_ref[...]) * (1.0 + sc2_ref[0]) + sh2_ref[0]).astype(BF16)
    hid = jnp.dot(xn, w1_ref[...], preferred_element_type=F32)
    hid = jnp.square(jnp.maximum(hid, 0.0)).astype(BF16)
    x2 = x1 + g2_ref[0] * jnp.dot(hid, w2_ref[...], preferred_element_type=F32)
    o_ref[0] = _rms(x2, fg_ref[...])


def _final(x, da, hf, hb, mo, g1, sh2, sc2, g2, mlg, n2g, fg, wo, w1, w2):
    bsz, t, d = x.shape
    tm = FINAL_TILE
    tok = lambda w: pl.BlockSpec((1, tm, w), lambda b, i: (b, i, 0))
    mod = pl.BlockSpec((1, 1, d), lambda b, i: (b, 0, 0))
    const = lambda a: pl.BlockSpec(a.shape, lambda b, i: (0, 0), pipeline_mode=pl.Buffered(1))
    return pl.pallas_call(
        _final_kernel,
        grid=(bsz, t // tm),
        in_specs=[tok(d), tok(512), tok(512), tok(512), tok(512),
                  mod, mod, mod, mod, const(mlg), const(n2g), const(fg), const(wo), const(w1), const(w2)],
        out_specs=tok(d),
        out_shape=jax.ShapeDtypeStruct((bsz, t, d), F32),
        compiler_params=pltpu.CompilerParams(
            dimension_semantics=("arbitrary", "arbitrary"), vmem_limit_bytes=VMEM_LIMIT_BYTES),
        name="final",
    )(x, da, hf, hb, mo, g1, sh2, sc2, g2, mlg, n2g, fg, wo, w1, w2)


def _rope_tables(t, nctx):
    rows = t // GRID_W
    row = jnp.repeat(jnp.arange(rows, dtype=F32), GRID_W)
    col = jnp.tile(jnp.arange(GRID_W, dtype=F32), rows)
    half = DA_QK // 2
    inv = ROPE_BASE ** (-jnp.arange(0, half, 2, dtype=F32) / half)
    ar = row[:, None] * inv
    ac = col[:, None] * inv
    ang = jnp.concatenate([ar, ar, ac, ac], axis=-1)
    sign = jnp.where(jnp.arange(DA_QK) % 32 < 16, -1.0, 1.0).astype(F32)
    cos = jnp.concatenate([jnp.ones((nctx, DA_QK), F32), jnp.cos(ang)], axis=0)
    sin = jnp.concatenate([jnp.zeros((nctx, DA_QK), F32), jnp.sin(ang) * sign], axis=0)
    cos = jnp.tile(cos, (1, 2))
    sin = jnp.tile(sin, (1, 2))
    return cos, sin


def _split_weights(w_in_l, b_gate_l):
    nq = DA_HEADS * 2 * DA_QK
    o = 0
    wdq = w_in_l[:, o:o + nq]; o += nq
    wdk = w_in_l[:, o:o + nq]; o += nq
    wdv = w_in_l[:, o:o + DA_HEADS * DA_V]; o += DA_HEADS * DA_V
    wmq = w_in_l[:, o:o + ML_HEADS * ML_QK]; o += ML_HEADS * ML_QK
    wmk = w_in_l[:, o:o + ML_HEADS * ML_QK]; o += ML_HEADS * ML_QK
    wmv = w_in_l[:, o:o + ML_HEADS * ML_V]; o += ML_HEADS * ML_V
    wmo = w_in_l[:, o:o + ML_HEADS * ML_V]; o += ML_HEADS * ML_V
    wmg = w_in_l[:, o:o + 4 * ML_HEADS]
    perm = jnp.array([0, 1, 2, 3, 8, 9, 10, 11, 4, 5, 6, 7, 12, 13, 14, 15])
    wmg = wmg[:, perm]
    bg = b_gate_l[perm].reshape(16, 1).astype(F32)
    qscale = DA_QK ** -0.5 * LOG2_E
    kscale = ML_QK ** -0.5
    wn = jnp.concatenate([wdk, wmk * kscale, wmo], axis=1).astype(BF16)
    wt = jnp.concatenate([wdq * qscale, wdv, wmq, wmv, wmg], axis=1).T.astype(BF16)
    return wn, wt, bg


def kernel(x, c, ctx, c_ctx, w_ada, b_ada, norm1_g, norm2_g, w_in, b_gate, lam_q1, lam_k1, lam_q2, lam_k2,
           subln_g, mlstm_norm_g, w_out, w_fc1, w_fc2, final_g):
    bsz, t, d = x.shape
    nctx = ctx.shape[1]
    depth = w_ada.shape[0]
    assert depth == 1, "single-layer block: the context stream is never updated"
    assert TOK_TILE == ML_CHUNK_LEN, "the projection computes per-chunk gate sums on its own tile"
    assert nctx % TOK_TILE == 0 and ATT_TQ % TOK_TILE == 0 and ATT_TK % TOK_TILE == 0
    assert t % ATT_TQ == 0 and t % ML_CHUNK_LEN == 0 and (nctx + t) % ATT_TK == 0 and t % GRID_W == 0
    assert t % FINAL_TILE == 0
    assert (t // ATT_TQ) * ((nctx + t) // ATT_TK) >= 3, "the attention pipeline needs at least three steps"
    lambda_init = 0.2

    cc = jnp.concatenate([c, c_ctx[None, :], jnp.zeros((8 - bsz - 1, d), F32)], axis=0)
    mod = _adaln(cc, w_ada[0], b_ada[0][None, :])
    mb = mod[:bsz].reshape(bsz, 1, 6, d)
    sh1, sc1, g1, sh2, sc2, g2 = [mb[:, :, k] for k in range(6)]
    mc = mod[bsz:bsz + 1].reshape(1, 1, 6, d)
    csh1, csc1 = mc[:, :, 0], mc[:, :, 1]

    wn, wt, bg = _split_weights(w_in[0], b_gate[0])
    cosn, sinn = _rope_tables(t, nctx)
    ka, km, mo, qat, vat, qmt, vmt, gt, cc = _project(
        x, ctx, sc1, sh1, csc1, csh1, norm1_g[0][None, :], wn, wt, bg, cosn, sinn)

    lamv = jnp.stack([lam_q1[0], lam_k1[0], lam_q2[0], lam_k2[0]]).astype(F32)
    da = _attention(qat, ka, vat, lamv, subln_g[0].reshape(DA_V, 1), t, lambda_init)
    hf, hb = _mlstm(qmt, km, vmt, gt, cc, t)

    return _final(x, da, hf, hb, mo, g1, sh2, sc2, g2, mlstm_norm_g[0][None, :], norm2_g[0][None, :],
                  final_g[None, :], w_out[0].astype(BF16), w_fc1[0].astype(BF16), w_fc2[0].astype(BF16))
```

```python
import functools

import jax
import jax.numpy as jnp
from jax import lax
from jax.experimental import pallas as pl
from jax.experimental.pallas import tpu as pltpu

DA_HEADS = 4
DA_QK = 64
DA_V = 128
ML_HEADS = 4
ML_QK = 64
ML_V = 128
GRID_W = 64
N_GATES = 4 * ML_HEADS
GATE_ROWS = 3 * 2 * ML_HEADS
ROPE_BASE = 10000.0
EPS = 1e-6
LOG2_E = 1.4426950408889634

LANES = 128
DA_K_COLS = DA_HEADS * 2 * DA_QK
DA_V_COLS = DA_HEADS * DA_V
ML_QK_COLS = ML_HEADS * ML_QK
ML_V_COLS = ML_HEADS * ML_V
HEAD_LANES = 2 * DA_QK
assert HEAD_LANES == LANES and DA_V == LANES and ML_V == LANES and 2 * ML_QK == LANES

TOK_TILE = 256
FINAL_TILE = 512
ATT_TQ = 512
ATT_TK = 768
ATT_MAX_UNROLL = 2
ML_CHUNK_LEN = 256
ML_STATE_ROWS = 144
VMEM_LIMIT_BYTES = 56 * 1024 * 1024

F32 = jnp.float32
BF16 = jnp.bfloat16


def _rms(x, g):
    return x * lax.rsqrt(jnp.mean(x * x, axis=-1, keepdims=True) + EPS) * g


def _adaln_kernel(c_ref, w_ref, b_ref, o_ref):
    c = c_ref[...]
    s = (c * jax.nn.sigmoid(c)).astype(BF16)
    o_ref[...] = jnp.dot(s, w_ref[...].astype(BF16), preferred_element_type=F32) + b_ref[...]


def _adaln(cc, w, b):
    rows, d = cc.shape
    n = w.shape[1]
    bn = d
    return pl.pallas_call(
        _adaln_kernel,
        grid=(n // bn,),
        in_specs=[pl.BlockSpec((rows, d), lambda j: (0, 0)),
                  pl.BlockSpec((d, bn), lambda j: (0, j)),
                  pl.BlockSpec((1, bn), lambda j: (0, j))],
        out_specs=pl.BlockSpec((rows, bn), lambda j: (0, j)),
        out_shape=jax.ShapeDtypeStruct((rows, n), F32),
        name="adaln",
    )(cc, w, b)


def _proj_kernel(x_ref, ctx_ref, sc_ref, sh_ref, csc_ref, csh_ref, g_ref, wn_ref, wt_ref, bg_ref,
                 cosn_ref, sinn_ref,
                 ka_ref, km_ref, mo_ref, qat_ref, vat_ref, qmt_ref, vmt_ref, gt_ref, cc_ref, *, nct):
    i = pl.program_id(0)
    is_ctx = i < nct
    x = jnp.where(is_ctx, ctx_ref[0], x_ref[0])
    sc = jnp.where(is_ctx, csc_ref[0], sc_ref[0])
    sh = jnp.where(is_ctx, csh_ref[0], sh_ref[0])
    xm = _rms(x, g_ref[...]) * (1.0 + sc) + sh
    xb = xm.astype(BF16)
    nt = (((1,), (1,)), ((), ()))
    ngate = DA_K_COLS + DA_V_COLS + ML_QK_COLS + ML_V_COLS

    g = lax.dot_general(wt_ref[ngate:ngate + N_GATES, :], xb, nt, preferred_element_type=F32) + bg_ref[...]
    li8 = g[0:8]
    lf8 = jnp.minimum(g[8:16], 0.0) - jnp.log1p(jnp.exp(-jnp.abs(g[8:16])))
    tm = g.shape[1]
    row8 = lax.broadcasted_iota(jnp.int32, (8, tm), 0)
    lane8 = lax.broadcasted_iota(jnp.int32, (8, tm), 1)
    is_f = row8 < ML_HEADS

    def scan(x, op, fill):
        step = 1
        while step < tm:
            prev = jnp.where(lane8 >= step, pltpu.roll(x, step, axis=1), fill)
            nxt = jnp.where(lane8 < tm - step, pltpu.roll(x, tm - step, axis=1), fill)
            x = op(x, jnp.where(is_f, prev, nxt))
            step *= 2
        return x

    b8 = scan(lf8, jnp.add, 0.0)
    c8 = li8 - b8
    cm8 = scan(c8, jnp.maximum, -jnp.inf)
    gt_ref[0] = jnp.concatenate([b8, c8, cm8], axis=0) * LOG2_E
    cc_ref[0] = (jnp.concatenate([c8, jnp.zeros((LANES - 8, tm), F32)], axis=0) * LOG2_E).T

    hn = jnp.dot(xb, wn_ref[...], preferred_element_type=F32)
    ht = lax.dot_general(wt_ref[0:ngate, :], xb, nt, preferred_element_type=F32)

    cosn, sinn = cosn_ref[...], sinn_ref[...]
    cost, sint = cosn.T, sinn.T
    lane = lax.broadcasted_iota(jnp.int32, (tm, HEAD_LANES), 1)
    low_half = (lane % 32) < 16
    for h in range(DA_HEADS):
        lo, hi = h * HEAD_LANES, (h + 1) * HEAD_LANES
        k = hn[:, lo:hi]
        kswap = jnp.where(low_half, pltpu.roll(k, HEAD_LANES - 16, axis=1), pltpu.roll(k, 16, axis=1))
        ka_ref[0, :, lo:hi] = (k * cosn + kswap * sinn).astype(BF16)
        q = ht[lo:hi]
        qswap = jnp.concatenate([q[r0 + 16:r0 + 32] if part == 0 else q[r0:r0 + 16]
                                 for r0 in range(0, HEAD_LANES, 32) for part in (0, 1)], axis=0)
        q = (q * cost + qswap * sint).astype(BF16)
        qrow = lax.broadcasted_iota(jnp.int32, q.shape, 0)
        zero = jnp.zeros_like(q)
        qat_ref[0, h, 0] = jnp.concatenate(
            [jnp.where(qrow < DA_QK, q, zero), jnp.where(qrow >= DA_QK, q, zero)], axis=1)
    km_ref[0] = hn[:, DA_K_COLS:DA_K_COLS + ML_QK_COLS].astype(BF16)
    mo_ref[0] = jax.nn.sigmoid(hn[:, DA_K_COLS + ML_QK_COLS:DA_K_COLS + ML_QK_COLS + ML_V_COLS]).astype(BF16)

    o = DA_K_COLS
    vat_ref[0, 0] = ht[o:o + DA_V_COLS].astype(BF16)
    o += DA_V_COLS
    qmt_ref[0] = ht[o:o + ML_QK_COLS].astype(BF16)
    o += ML_QK_COLS
    vmt_ref[0] = ht[o:o + ML_V_COLS].astype(BF16)


def _project(x, ctx, sc, sh, csc, csh, g, wn, wt, bg, cosn, sinn):
    bsz, t, d = x.shape
    nctx = ctx.shape[1]
    n = nctx + t
    tm = TOK_TILE
    nct = nctx // tm
    r = ATT_TK // tm
    const2 = lambda i, b: (0, 0)
    const3 = lambda i, b: (0, 0, 0)
    in_specs = [
        pl.BlockSpec((1, tm, d), lambda i, b: (b, jnp.maximum(i - nct, 0), 0)),
        pl.BlockSpec((1, tm, d), lambda i, b: (b, jnp.minimum(i, nct - 1), 0)),
        pl.BlockSpec((1, 1, d), lambda i, b: (b, 0, 0)),
        pl.BlockSpec((1, 1, d), lambda i, b: (b, 0, 0)),
        pl.BlockSpec((1, 1, d), const3),
        pl.BlockSpec((1, 1, d), const3),
        pl.BlockSpec((1, d), const2),
        pl.BlockSpec(wn.shape, const2),
        pl.BlockSpec(wt.shape, const2),
        pl.BlockSpec(bg.shape, const2),
        pl.BlockSpec((tm, HEAD_LANES), lambda i, b: (i, 0)),
        pl.BlockSpec((tm, HEAD_LANES), lambda i, b: (i, 0)),
    ]
    out_specs = [
        pl.BlockSpec((1, tm, DA_K_COLS), lambda i, b: (b, i, 0)),
        pl.BlockSpec((1, tm, ML_QK_COLS), lambda i, b: (b, i, 0)),
        pl.BlockSpec((1, tm, ML_V_COLS), lambda i, b: (b, jnp.where(i < nct, n // tm - nct + i, i - nct), 0)),
        pl.BlockSpec((1, DA_HEADS, 1, HEAD_LANES, 2 * tm), lambda i, b: (b, 0, i, 0, 0)),
        pl.BlockSpec((1, 1, DA_V_COLS, tm), lambda i, b: (b, i // r, 0, i % r)),
        pl.BlockSpec((1, ML_QK_COLS, tm), lambda i, b: (b, 0, i)),
        pl.BlockSpec((1, ML_V_COLS, tm), lambda i, b: (b, 0, i)),
        pl.BlockSpec((1, GATE_ROWS, tm), lambda i, b: (b, 0, i)),
        pl.BlockSpec((1, tm, LANES), lambda i, b: (b, i, 0)),
    ]
    out_shape = [
        jax.ShapeDtypeStruct((bsz, n, DA_K_COLS), BF16),
        jax.ShapeDtypeStruct((bsz, n, ML_QK_COLS), BF16),
        jax.ShapeDtypeStruct((bsz, n, ML_V_COLS), BF16),
        jax.ShapeDtypeStruct((bsz, DA_HEADS, n // tm, HEAD_LANES, 2 * tm), BF16),
        jax.ShapeDtypeStruct((bsz, n // ATT_TK, DA_V_COLS, ATT_TK), BF16),
        jax.ShapeDtypeStruct((bsz, ML_QK_COLS, n), BF16),
        jax.ShapeDtypeStruct((bsz, ML_V_COLS, n), BF16),
        jax.ShapeDtypeStruct((bsz, GATE_ROWS, n), F32),
        jax.ShapeDtypeStruct((bsz, n, LANES), F32),
    ]
    return pl.pallas_call(
        functools.partial(_proj_kernel, nct=nct),
        grid=(n // tm, bsz),
        in_specs=in_specs,
        out_specs=out_specs,
        out_shape=out_shape,
        compiler_params=pltpu.CompilerParams(
            dimension_semantics=("arbitrary", "arbitrary"), vmem_limit_bytes=VMEM_LIMIT_BYTES),
        name="project",
    )(x, ctx, sc, sh, csc, csh, g, wn, wt, bg, cosn, sinn)


def _attn_kernel(q_ref, k_ref, v_ref, lam_ref, g_ref, o_ref, m_s, l_s, lfin_s, acc_s, s_s, mb_s, al_s, p_s, *,
                 tq, tk, nblk, ntile, qsub, qoff, unroll, lambda_init):
    nstep = ntile * nblk
    sw = q_ref.shape[-1]
    half = sw // 2

    m_s[...] = jnp.zeros(m_s.shape, F32)
    l_s[...] = jnp.zeros(l_s.shape, F32)
    lfin_s[...] = jnp.ones(lfin_s.shape, F32)
    acc_s[...] = jnp.zeros(acc_s.shape, F32)

    lv = lam_ref[...]
    lam = (jnp.exp(jnp.sum(lv[0:1] * lv[1:2], axis=1, keepdims=True))
           - jnp.exp(jnp.sum(lv[2:3] * lv[3:4], axis=1, keepdims=True)) + lambda_init)

    def values(u):
        kb = lax.rem(u, nblk)
        slot = lax.rem(lax.div(u, nblk), 2)
        pv = jnp.dot(v_ref[0, kb], p_s[...], preferred_element_type=F32)
        acc_s[slot] = al_s[...] * acc_s[slot] + pv

    def scale(u):
        kb = lax.rem(u, nblk)
        first = kb == 0
        m_old = jnp.where(first, -jnp.inf, m_s[...])
        l_old = jnp.where(first, 0.0, l_s[...])
        m_new = jnp.maximum(m_old, mb_s[...])
        alpha = jnp.exp2(m_old - m_new)
        p = jnp.exp2(s_s[...] - m_new)
        l_new = alpha * l_old + jnp.sum(p, axis=0, keepdims=True)
        l_s[...] = l_new
        lfin_s[...] = jnp.where(kb == nblk - 1, l_new, lfin_s[...])
        p_s[...] = p.astype(BF16)
        al_s[...] = alpha
        m_s[...] = m_new

    def scores(u):
        tile = lax.div(u, nblk)
        kb = lax.rem(u, nblk)
        qbd = jnp.concatenate([q_ref[0, 0, qoff + tile * qsub + c] for c in range(qsub)], axis=1)
        k = k_ref[0, pl.ds(pl.multiple_of(kb * tk, tk), tk), :]
        s = jnp.dot(k, qbd, preferred_element_type=F32)
        s_s[...] = s
        mb_s[...] = jnp.max(s, axis=0, keepdims=True)

    def finalize(tile):
        inv_l = 1.0 / lfin_s[...]
        acc = acc_s[lax.rem(tile, 2)]
        for c in range(qsub):
            a1, a2 = c * sw, c * sw + half
            o = acc[:, a1:a2] * inv_l[:, a1:a2] - lam * (acc[:, a2:a2 + half] * inv_l[:, a2:a2 + half])
            y = o * lax.rsqrt(jnp.mean(o * o, axis=0, keepdims=True) + EPS) * g_ref[...] * (1.0 - lambda_init)
            o_ref[0, pl.ds(pl.multiple_of(tile * tq + c * half, half), half), :] = y.T.astype(BF16)

    scores(0)
    scale(0)
    scores(1)

    def body(i, carry):
        u = unroll * i + 1
        for d in range(unroll):
            values(u - 1 + d)
            scale(u + d)
            scores(u + 1 + d)
        for d in range(unroll):
            w = u - 1 + d

            @pl.when(lax.rem(w, nblk) == nblk - 1)
            def _():
                finalize(lax.div(w, nblk))
        return carry

    assert (nstep - 2) % unroll == 0 and nblk >= unroll
    lax.fori_loop(0, (nstep - 2) // unroll, body, 0)
    values(nstep - 2)
    scale(nstep - 1)
    values(nstep - 1)
    finalize(ntile - 1)


def _attn_unroll(loop_steps, nblk):
    return max(u for u in range(1, min(ATT_MAX_UNROLL, nblk) + 1) if loop_steps % u == 0)


def _attention(qat, ka, vat, lamv, subg, t, lambda_init):
    bsz, _, nsub, _, sw = qat.shape
    n = ka.shape[1]
    tq, tk = ATT_TQ, ATT_TK
    sub = sw // 2
    nblk = n // tk
    kern = functools.partial(_attn_kernel, tq=tq, tk=tk, nblk=nblk, ntile=t // tq, qsub=tq // sub,
                             qoff=(n - t) // sub, unroll=_attn_unroll((t // tq) * nblk - 2, nblk), lambda_init=lambda_init)
    row = lambda: pltpu.VMEM((1, 2 * tq), F32)
    return pl.pallas_call(
        kern,
        grid=(bsz, DA_HEADS),
        in_specs=[
            pl.BlockSpec((1, 1, nsub, HEAD_LANES, sw), lambda b, h: (b, h, 0, 0, 0)),
            pl.BlockSpec((1, n, HEAD_LANES), lambda b, h: (b, 0, h)),
            pl.BlockSpec((1, nblk, DA_V, tk), lambda b, h: (b, 0, h, 0)),
            pl.BlockSpec(lamv.shape, lambda b, h: (0, 0)),
            pl.BlockSpec(subg.shape, lambda b, h: (0, 0)),
        ],
        out_specs=pl.BlockSpec((1, t, DA_V), lambda b, h: (b, 0, h)),
        out_shape=jax.ShapeDtypeStruct((bsz, t, DA_HEADS * DA_V), BF16),
        scratch_shapes=[row(), row(), row(), pltpu.VMEM((2, DA_V, 2 * tq), F32), pltpu.VMEM((tk, 2 * tq), F32),
                        row(), row(), pltpu.VMEM((tk, 2 * tq), BF16)],
        compiler_params=pltpu.CompilerParams(
            dimension_semantics=("arbitrary", "arbitrary"), vmem_limit_bytes=VMEM_LIMIT_BYTES),
        name="diff_attn",
    )(qat, ka, vat, lamv, subg)


def _mlstm_kernel(qf_ref, kf_ref, vf_ref, gf_ref, cf_ref, qb_ref, kb_ref, vb_ref, gb_ref, cb_ref,
                  hf_ref, hb_ref, c_s, m_s, *, L):
    j = pl.program_id(1)

    @pl.when(j == 0)
    def _():
        c_s[...] = jnp.zeros(c_s.shape, F32)
        m_s[...] = jnp.zeros(m_s.shape, F32)

    sidx = lax.broadcasted_iota(jnp.int32, (L, L), 0)
    tidx = lax.broadcasted_iota(jnp.int32, (L, L), 1)
    row128 = lax.broadcasted_iota(jnp.int32, (LANES, L), 0)
    lane128 = lax.broadcasted_iota(jnp.int32, (ML_STATE_ROWS, LANES), 1)
    row16 = lax.broadcasted_iota(jnp.int32, (16, L), 0)

    chains = range(2 * ML_HEADS)
    rev = [r >= ML_HEADS for r in chains]
    head = [r % ML_HEADS for r in chains]
    refs = [(qb_ref, kb_ref, vb_ref, gb_ref, cb_ref) if rev[r] else (qf_ref, kf_ref, vf_ref, gf_ref, cf_ref)
            for r in chains]
    pairs = [slice((head[r] // 2) * LANES, (head[r] // 2 + 1) * LANES) for r in chains]
    odd = [head[r] % 2 == 1 for r in chains]

    qpair = [refs[r][0][0, pairs[r], :] for r in chains]
    kpair = [refs[r][1][0, :, pairs[r]] for r in chains]
    vt = [refs[r][2][0, head[r] * ML_V:(head[r] + 1) * ML_V, :] for r in chains]
    b_row = [refs[r][3][0, r:r + 1, :] for r in chains]
    c_row = [refs[r][3][0, 8 + r:9 + r, :] for r in chains]
    cm_row = [refs[r][3][0, 16 + r:17 + r, :] for r in chains]
    c_col = [refs[r][4][0, :, r:r + 1] for r in chains]
    m_prev = [m_s[r] for r in chains]
    state = [c_s[r] for r in chains]
    m_run = [jnp.maximum(m_prev[r], cm_row[r]) for r in chains]

    qt = [jnp.where((row128 >= ML_QK) if odd[r] else (row128 < ML_QK), qpair[r], jnp.zeros_like(qpair[r]))
          for r in chains]
    st = [jnp.dot(kpair[r], qt[r], preferred_element_type=F32) for r in chains]
    cq = [jnp.dot(state[r].astype(BF16), qpair[r], preferred_element_type=F32) for r in chains]
    sp = [jnp.where((sidx >= tidx) if rev[r] else (sidx <= tidx),
                    st[r] * jnp.exp2(c_col[r] - m_run[r]), 0.0) for r in chains]
    den_intra = [jnp.sum(sp[r], axis=0, keepdims=True) for r in chains]
    num_intra = [jnp.dot(vt[r], sp[r].astype(BF16), preferred_element_type=F32) for r in chains]

    b_tot = [b_row[r][:, 0:1] if rev[r] else b_row[r][:, L - 1:L] for r in chains]
    cm_end = [cm_row[r][:, 0:1] if rev[r] else cm_row[r][:, L - 1:L] for r in chains]
    m_end = [jnp.maximum(m_prev[r][:, 0:1], cm_end[r]) for r in chains]
    w_row = [jnp.exp2(c_row[r] - m_end[r]) for r in chains]
    aug = [jnp.concatenate([vt[r].astype(F32) * w_row[r],
                            jnp.where(row16 == 0, jnp.broadcast_to(w_row[r], (16, L)), 0.0)], axis=0)
           for r in chains]
    upd = [jnp.dot(aug[r].astype(BF16), kpair[r], preferred_element_type=F32) for r in chains]

    for r in chains:
        w_inter = jnp.exp2(m_prev[r] - m_run[r])
        num = num_intra[r] + w_inter * cq[r][0:ML_V]
        den = den_intra[r] + w_inter * cq[r][ML_V:ML_V + 1]
        ht = num * (1.0 / jnp.maximum(jnp.abs(den), jnp.exp2(-(b_row[r] + m_run[r]))))
        out_ref = hb_ref if rev[r] else hf_ref
        out_ref[0, :, head[r] * ML_V:(head[r] + 1) * ML_V] = ht.T
        own_lanes = (lane128 >= ML_QK) if odd[r] else (lane128 < ML_QK)
        c_s[r] = jnp.exp2(m_prev[r][:, 0:1] - m_end[r]) * state[r] + jnp.where(own_lanes, upd[r], 0.0)
        m_s[r] = jnp.broadcast_to(b_tot[r] + m_end[r], (1, L))


def _mlstm(qmt, km, vmt, gt, cc, t):
    bsz, _, n = qmt.shape
    L = ML_CHUNK_LEN
    nch = n // L
    nctx_ch = (n - t) // L
    nlat = t // L

    def fwd_c(j):
        return j

    def bwd_c(j):
        return jnp.where(j < nctx_ch, nctx_ch - 1 - j, nch - 1 - (j - nctx_ch))

    def specs(cfn):
        return [
            pl.BlockSpec((1, ML_QK_COLS, L), lambda b, j: (b, 0, cfn(j))),
            pl.BlockSpec((1, L, ML_QK_COLS), lambda b, j: (b, cfn(j), 0)),
            pl.BlockSpec((1, ML_V_COLS, L), lambda b, j: (b, 0, cfn(j))),
            pl.BlockSpec((1, GATE_ROWS, L), lambda b, j: (b, 0, cfn(j))),
            pl.BlockSpec((1, L, LANES), lambda b, j: (b, cfn(j), 0)),
        ]

    lat = lambda j: jnp.maximum(j - nctx_ch, 0)
    out_specs = [
        pl.BlockSpec((1, L, ML_V_COLS), lambda b, j: (b, lat(j), 0)),
        pl.BlockSpec((1, L, ML_V_COLS), lambda b, j: (b, nlat - 1 - lat(j), 0)),
    ]
    out_shape = [jax.ShapeDtypeStruct((bsz, t, ML_V_COLS), F32)] * 2
    return pl.pallas_call(
        functools.partial(_mlstm_kernel, L=L),
        grid=(bsz, nch),
        in_specs=specs(fwd_c) + specs(bwd_c),
        out_specs=out_specs,
        out_shape=out_shape,
        scratch_shapes=[pltpu.VMEM((2 * ML_HEADS, ML_STATE_ROWS, LANES), F32),
                        pltpu.VMEM((2 * ML_HEADS, 1, L), F32)],
        compiler_params=pltpu.CompilerParams(
            dimension_semantics=("arbitrary", "arbitrary"), vmem_limit_bytes=VMEM_LIMIT_BYTES),
        name="mlstm",
    )(qmt, km, vmt, gt, cc, qmt, km, vmt, gt, cc)


def _final_kernel(x_ref, da_ref, hf_ref, hb_ref, mo_ref, g1_ref, sh2_ref, sc2_ref, g2_ref,
                  mlg_ref, n2g_ref, fg_ref, wo_ref, w1_ref, w2_ref, o_ref):
    x = x_ref[0]
    hsum = hf_ref[0] + hb_ref[0]
    parts = []
    for h in range(ML_HEADS):
        u = hsum[:, h * ML_V:(h + 1) * ML_V]
        parts.append(u * lax.rsqrt(jnp.mean(u * u, axis=1, keepdims=True) + EPS))
    ml = jnp.concatenate(parts, axis=1) * mlg_ref[...] * mo_ref[0].astype(F32)
    cat = jnp.concatenate([da_ref[0], ml.astype(BF16)], axis=1)
    y = jnp.dot(cat, wo_ref[...], preferred_element_type=F32)
    x1 = x + g1_ref[0] * y
    xn = (_rms(x1, n2g_ref[...]) * (1.0 + sc2_ref[0]) + sh2_ref[0]).astype(BF16)
    hid = jnp.dot(xn, w1_ref[...], preferred_element_type=F32)
    hid = jnp.square(jnp.maximum(hid, 0.0)).astype(BF16)
    x2 = x1 + g2_ref[0] * jnp.dot(hid, w2_ref[...], preferred_element_type=F32)
    o_ref[0] = _rms(x2, fg_ref[...])


def _final(x, da, hf, hb, mo, g1, sh2, sc2, g2, mlg, n2g, fg, wo, w1, w2):
    bsz, t, d = x.shape
    tm = FINAL_TILE
    tok = lambda w: pl.BlockSpec((1, tm, w), lambda b, i: (b, i, 0))
    mod = pl.BlockSpec((1, 1, d), lambda b, i: (b, 0, 0))
    const = lambda a: pl.BlockSpec(a.shape, lambda b, i: (0, 0), pipeline_mode=pl.Buffered(1))
    return pl.pallas_call(
        _final_kernel,
        grid=(bsz, t // tm),
        in_specs=[tok(d), tok(DA_V_COLS), tok(ML_V_COLS), tok(ML_V_COLS), tok(ML_V_COLS),
                  mod, mod, mod, mod, const(mlg), const(n2g), const(fg), const(wo), const(w1), const(w2)],
        out_specs=tok(d),
        out_shape=jax.ShapeDtypeStruct((bsz, t, d), F32),
        compiler_params=pltpu.CompilerParams(
            dimension_semantics=("arbitrary", "arbitrary"), vmem_limit_bytes=VMEM_LIMIT_BYTES),
        name="final",
    )(x, da, hf, hb, mo, g1, sh2, sc2, g2, mlg, n2g, fg, wo, w1, w2)


def _rope_tables(t, nctx):
    rows = t // GRID_W
    half = DA_QK // 2
    inv = ROPE_BASE ** (-jnp.arange(0, half, 2, dtype=F32) / half)
    ar = jnp.arange(rows, dtype=F32)[:, None] * inv
    ac = jnp.arange(GRID_W, dtype=F32)[:, None] * inv
    sign = jnp.where(jnp.arange(DA_QK) % 32 < 16, -1.0, 1.0).astype(F32)

    def table(fn, ctx_row):
        r = jnp.repeat(fn(ar), GRID_W, axis=0)
        c = jnp.tile(fn(ac), (rows, 1))
        lat = jnp.concatenate([r, r, c, c], axis=-1)
        return jnp.concatenate([jnp.broadcast_to(ctx_row, (nctx, DA_QK)), lat], axis=0)

    cos = table(jnp.cos, jnp.ones((DA_QK,), F32))
    sin = table(jnp.sin, jnp.zeros((DA_QK,), F32)) * sign
    return jnp.tile(cos, (1, 2)), jnp.tile(sin, (1, 2))


def _split_weights(w_in_l, b_gate_l):
    nq = DA_HEADS * 2 * DA_QK
    o = 0
    wdq = w_in_l[:, o:o + nq]; o += nq
    wdk = w_in_l[:, o:o + nq]; o += nq
    wdv = w_in_l[:, o:o + DA_HEADS * DA_V]; o += DA_HEADS * DA_V
    wmq = w_in_l[:, o:o + ML_HEADS * ML_QK]; o += ML_HEADS * ML_QK
    wmk = w_in_l[:, o:o + ML_HEADS * ML_QK]; o += ML_HEADS * ML_QK
    wmv = w_in_l[:, o:o + ML_HEADS * ML_V]; o += ML_HEADS * ML_V
    wmo = w_in_l[:, o:o + ML_HEADS * ML_V]; o += ML_HEADS * ML_V
    wmg = w_in_l[:, o:o + 4 * ML_HEADS]
    perm = jnp.array([0, 1, 2, 3, 8, 9, 10, 11, 4, 5, 6, 7, 12, 13, 14, 15])
    wmg = wmg[:, perm]
    bg = b_gate_l[perm].reshape(16, 1).astype(F32)
    qscale = DA_QK ** -0.5 * LOG2_E
    kscale = ML_QK ** -0.5
    wn = jnp.concatenate([wdk, wmk * kscale, wmo], axis=1).astype(BF16)
    wt = jnp.concatenate([wdq * qscale, wdv, wmq, wmv, wmg], axis=1).T.astype(BF16)
    return wn, wt, bg


def kernel(x, c, ctx, c_ctx, w_ada, b_ada, norm1_g, norm2_g, w_in, b_gate, lam_q1, lam_k1, lam_q2, lam_k2,
           subln_g, mlstm_norm_g, w_out, w_fc1, w_fc2, final_g):
    bsz, t, d = x.shape
    nctx = ctx.shape[1]
    depth = w_ada.shape[0]
    assert depth == 1, "single-layer block: the context stream is never updated"
    assert TOK_TILE == ML_CHUNK_LEN, "the projection computes per-chunk gate sums on its own tile"
    assert nctx % TOK_TILE == 0 and ATT_TQ % TOK_TILE == 0 and ATT_TK % TOK_TILE == 0
    assert t % ATT_TQ == 0 and t % ML_CHUNK_LEN == 0 and (nctx + t) % ATT_TK == 0 and t % GRID_W == 0
    assert t % FINAL_TILE == 0
    assert (t // ATT_TQ) * ((nctx + t) // ATT_TK) >= 3, "the attention pipeline needs at least three steps"
    lambda_init = 0.2

    cc = jnp.concatenate([c, c_ctx[None, :], jnp.zeros((8 - bsz - 1, d), F32)], axis=0)
    mod = _adaln(cc, w_ada[0], b_ada[0][None, :])
    mb = mod[:bsz].reshape(bsz, 1, 6, d)
    sh1, sc1, g1, sh2, sc2, g2 = [mb[:, :, k] for k in range(6)]
    mc = mod[bsz:bsz + 1].reshape(1, 1, 6, d)
    csh1, csc1 = mc[:, :, 0], mc[:, :, 1]

    wn, wt, bg = _split_weights(w_in[0], b_gate[0])
    cosn, sinn = _rope_tables(t, nctx)
    ka, km, mo, qat, vat, qmt, vmt, gt, cc = _project(
        x, ctx, sc1, sh1, csc1, csh1, norm1_g[0][None, :], wn, wt, bg, cosn, sinn)

    lamv = jnp.stack([lam_q1[0], lam_k1[0], lam_q2[0], lam_k2[0]]).astype(F32)
    da = _attention(qat, ka, vat, lamv, subln_g[0].reshape(DA_V, 1), t, lambda_init)
    hf, hb = _mlstm(qmt, km, vmt, gt, cc, t)

    return _final(x, da, hf, hb, mo, g1, sh2, sc2, g2, mlstm_norm_g[0][None, :], norm2_g[0][None, :],
                  final_g[None, :], w_out[0].astype(BF16), w_fc1[0].astype(BF16), w_fc2[0].astype(BF16))
```

```python
import functools

import jax
import jax.numpy as jnp
from jax import lax
from jax.experimental import pallas as pl
from jax.experimental.pallas import tpu as pltpu

DA_HEADS = 4
DA_QK = 64
DA_V = 128
ML_HEADS = 4
ML_QK = 64
ML_V = 128
GRID_W = 64
N_GATES = 4 * ML_HEADS
GATE_ROWS = 3 * 2 * ML_HEADS
ROPE_BASE = 10000.0
EPS = 1e-6
LOG2_E = 1.4426950408889634

LANES = 128
DA_K_COLS = DA_HEADS * 2 * DA_QK
DA_V_COLS = DA_HEADS * DA_V
ML_QK_COLS = ML_HEADS * ML_QK
ML_V_COLS = ML_HEADS * ML_V
HEAD_LANES = 2 * DA_QK
assert HEAD_LANES == LANES and DA_V == LANES and ML_V == LANES and 2 * ML_QK == LANES

TOK_TILE = 256
FINAL_TILE = 512
ATT_TQ = 512
ATT_TK = 768
ATT_MAX_UNROLL = 2
ML_CHUNK_LEN = 256
ML_STATE_ROWS = 144
VMEM_LIMIT_BYTES = 56 * 1024 * 1024

F32 = jnp.float32
BF16 = jnp.bfloat16


def _rms(x, g):
    return x * lax.rsqrt(jnp.mean(x * x, axis=-1, keepdims=True) + EPS) * g


def _adaln_kernel(c_ref, w_ref, b_ref, o_ref):
    c = c_ref[...]
    s = (c * jax.nn.sigmoid(c)).astype(BF16)
    o_ref[...] = jnp.dot(s, w_ref[...].astype(BF16), preferred_element_type=F32) + b_ref[...]


def _adaln(cc, w, b):
    rows, d = cc.shape
    n = w.shape[1]
    bn = d
    return pl.pallas_call(
        _adaln_kernel,
        grid=(n // bn,),
        in_specs=[pl.BlockSpec((rows, d), lambda j: (0, 0)),
                  pl.BlockSpec((d, bn), lambda j: (0, j)),
                  pl.BlockSpec((1, bn), lambda j: (0, j))],
        out_specs=pl.BlockSpec((rows, bn), lambda j: (0, j)),
        out_shape=jax.ShapeDtypeStruct((rows, n), F32),
        name="adaln",
    )(cc, w, b)


def _proj_kernel(x_ref, ctx_ref, sc_ref, sh_ref, csc_ref, csh_ref, g_ref, wn_ref, wt_ref, bg_ref,
                 cosn_ref, sinn_ref,
                 ka_ref, km_ref, mo_ref, qat_ref, vat_ref, qmt_ref, vmt_ref, gt_ref, cc_ref, *, nct):
    i = pl.program_id(0)
    is_ctx = i < nct
    x = jnp.where(is_ctx, ctx_ref[0], x_ref[0])
    sc = jnp.where(is_ctx, csc_ref[0], sc_ref[0])
    sh = jnp.where(is_ctx, csh_ref[0], sh_ref[0])
    xm = _rms(x, g_ref[...]) * (1.0 + sc) + sh
    xb = xm.astype(BF16)
    nt = (((1,), (1,)), ((), ()))
    ngate = DA_K_COLS + DA_V_COLS + ML_QK_COLS + ML_V_COLS

    g = lax.dot_general(wt_ref[ngate:ngate + N_GATES, :], xb, nt, preferred_element_type=F32) + bg_ref[...]
    li8 = g[0:8]
    lf8 = jnp.minimum(g[8:16], 0.0) - jnp.log1p(jnp.exp(-jnp.abs(g[8:16])))
    tm = g.shape[1]
    row8 = lax.broadcasted_iota(jnp.int32, (8, tm), 0)
    lane8 = lax.broadcasted_iota(jnp.int32, (8, tm), 1)
    is_f = row8 < ML_HEADS

    def scan(x, op, fill):
        step = 1
        while step < tm:
            prev = jnp.where(lane8 >= step, pltpu.roll(x, step, axis=1), fill)
            nxt = jnp.where(lane8 < tm - step, pltpu.roll(x, tm - step, axis=1), fill)
            x = op(x, jnp.where(is_f, prev, nxt))
            step *= 2
        return x

    b8 = scan(lf8, jnp.add, 0.0)
    c8 = li8 - b8
    cm8 = scan(c8, jnp.maximum, -jnp.inf)
    gt_ref[0] = jnp.concatenate([b8, c8, cm8], axis=0) * LOG2_E
    cc_ref[0] = (jnp.concatenate([c8, jnp.zeros((LANES - 8, tm), F32)], axis=0) * LOG2_E).T

    hn = jnp.dot(xb, wn_ref[...], preferred_element_type=F32)
    ht = lax.dot_general(wt_ref[0:ngate, :], xb, nt, preferred_element_type=F32)

    cosn, sinn = cosn_ref[...], sinn_ref[...]
    cost, sint = cosn.T, sinn.T
    lane = lax.broadcasted_iota(jnp.int32, (tm, HEAD_LANES), 1)
    low_half = (lane % 32) < 16
    for h in range(DA_HEADS):
        lo, hi = h * HEAD_LANES, (h + 1) * HEAD_LANES
        k = hn[:, lo:hi]
        kswap = jnp.where(low_half, pltpu.roll(k, HEAD_LANES - 16, axis=1), pltpu.roll(k, 16, axis=1))
        ka_ref[0, :, lo:hi] = (k * cosn + kswap * sinn).astype(BF16)
        q = ht[lo:hi]
        qswap = jnp.concatenate([q[r0 + 16:r0 + 32] if part == 0 else q[r0:r0 + 16]
                                 for r0 in range(0, HEAD_LANES, 32) for part in (0, 1)], axis=0)
        q = (q * cost + qswap * sint).astype(BF16)
        qrow = lax.broadcasted_iota(jnp.int32, q.shape, 0)
        zero = jnp.zeros_like(q)
        qat_ref[0, h, 0] = jnp.concatenate(
            [jnp.where(qrow < DA_QK, q, zero), jnp.where(qrow >= DA_QK, q, zero)], axis=1)
    km_ref[0] = hn[:, DA_K_COLS:DA_K_COLS + ML_QK_COLS].astype(BF16)
    mo_ref[0] = jax.nn.sigmoid(hn[:, DA_K_COLS + ML_QK_COLS:DA_K_COLS + ML_QK_COLS + ML_V_COLS]).astype(BF16)

    o = DA_K_COLS
    vat_ref[0, 0] = ht[o:o + DA_V_COLS].astype(BF16)
    o += DA_V_COLS
    qmt_ref[0] = ht[o:o + ML_QK_COLS].astype(BF16)
    o += ML_QK_COLS
    vmt_ref[0] = ht[o:o + ML_V_COLS].astype(BF16)


def _project(x, ctx, sc, sh, csc, csh, g, wn, wt, bg, cosn, sinn):
    bsz, t, d = x.shape
    nctx = ctx.shape[1]
    n = nctx + t
    tm = TOK_TILE
    nct = nctx // tm
    r = ATT_TK // tm
    const2 = lambda i, b: (0, 0)
    const3 = lambda i, b: (0, 0, 0)
    in_specs = [
        pl.BlockSpec((1, tm, d), lambda i, b: (b, jnp.maximum(i - nct, 0), 0)),
        pl.BlockSpec((1, tm, d), lambda i, b: (b, jnp.minimum(i, nct - 1), 0)),
        pl.BlockSpec((1, 1, d), lambda i, b: (b, 0, 0)),
        pl.BlockSpec((1, 1, d), lambda i, b: (b, 0, 0)),
        pl.BlockSpec((1, 1, d), const3),
        pl.BlockSpec((1, 1, d), const3),
        pl.BlockSpec((1, d), const2),
        pl.BlockSpec(wn.shape, const2),
        pl.BlockSpec(wt.shape, const2),
        pl.BlockSpec(bg.shape, const2),
        pl.BlockSpec((tm, HEAD_LANES), lambda i, b: (i, 0)),
        pl.BlockSpec((tm, HEAD_LANES), lambda i, b: (i, 0)),
    ]
    out_specs = [
        pl.BlockSpec((1, tm, DA_K_COLS), lambda i, b: (b, i, 0)),
        pl.BlockSpec((1, tm, ML_QK_COLS), lambda i, b: (b, i, 0)),
        pl.BlockSpec((1, tm, ML_V_COLS), lambda i, b: (b, jnp.where(i < nct, n // tm - nct + i, i - nct), 0)),
        pl.BlockSpec((1, DA_HEADS, 1, HEAD_LANES, 2 * tm), lambda i, b: (b, 0, i, 0, 0)),
        pl.BlockSpec((1, 1, DA_V_COLS, tm), lambda i, b: (b, i // r, 0, i % r)),
        pl.BlockSpec((1, ML_QK_COLS, tm), lambda i, b: (b, 0, i)),
        pl.BlockSpec((1, ML_V_COLS, tm), lambda i, b: (b, 0, i)),
        pl.BlockSpec((1, GATE_ROWS, tm), lambda i, b: (b, 0, i)),
        pl.BlockSpec((1, tm, LANES), lambda i, b: (b, i, 0)),
    ]
    out_shape = [
        jax.ShapeDtypeStruct((bsz, n, DA_K_COLS), BF16),
        jax.ShapeDtypeStruct((bsz, n, ML_QK_COLS), BF16),
        jax.ShapeDtypeStruct((bsz, n, ML_V_COLS), BF16),
        jax.ShapeDtypeStruct((bsz, DA_HEADS, n // tm, HEAD_LANES, 2 * tm), BF16),
        jax.ShapeDtypeStruct((bsz, n // ATT_TK, DA_V_COLS, ATT_TK), BF16),
        jax.ShapeDtypeStruct((bsz, ML_QK_COLS, n), BF16),
        jax.ShapeDtypeStruct((bsz, ML_V_COLS, n), BF16),
        jax.ShapeDtypeStruct((bsz, GATE_ROWS, n), F32),
        jax.ShapeDtypeStruct((bsz, n, LANES), F32),
    ]
    return pl.pallas_call(
        functools.partial(_proj_kernel, nct=nct),
        grid=(n // tm, bsz),
        in_specs=in_specs,
        out_specs=out_specs,
        out_shape=out_shape,
        compiler_params=pltpu.CompilerParams(
            dimension_semantics=("arbitrary", "arbitrary"), vmem_limit_bytes=VMEM_LIMIT_BYTES),
        name="project",
    )(x, ctx, sc, sh, csc, csh, g, wn, wt, bg, cosn, sinn)


def _attn_kernel(q_ref, k_ref, v_ref, lam_ref, o_ref, m_s, l_s, lfin_s, acc_s, s_s, mb_s, al_s, p_s, *,
                 tq, tk, nblk, ntile, qsub, qoff, unroll, lambda_init):
    nstep = ntile * nblk
    sw = q_ref.shape[-1]
    half = sw // 2

    m_s[...] = jnp.zeros(m_s.shape, F32)
    l_s[...] = jnp.zeros(l_s.shape, F32)
    lfin_s[...] = jnp.ones(lfin_s.shape, F32)
    acc_s[...] = jnp.zeros(acc_s.shape, F32)

    lv = lam_ref[...]
    lam = (jnp.exp(jnp.sum(lv[0:1] * lv[1:2], axis=1, keepdims=True))
           - jnp.exp(jnp.sum(lv[2:3] * lv[3:4], axis=1, keepdims=True)) + lambda_init)

    def values(u):
        kb = lax.rem(u, nblk)
        slot = lax.rem(lax.div(u, nblk), 2)
        pv = jnp.dot(v_ref[0, kb], p_s[...], preferred_element_type=F32)
        acc_s[slot] = al_s[...] * acc_s[slot] + pv

    def scale(u):
        kb = lax.rem(u, nblk)
        first = kb == 0
        m_old = jnp.where(first, -jnp.inf, m_s[...])
        l_old = jnp.where(first, 0.0, l_s[...])
        m_new = jnp.maximum(m_old, mb_s[...])
        alpha = jnp.exp2(m_old - m_new)
        p = jnp.exp2(s_s[...] - m_new)
        l_new = alpha * l_old + jnp.sum(p, axis=0, keepdims=True)
        l_s[...] = l_new
        lfin_s[...] = jnp.where(kb == nblk - 1, l_new, lfin_s[...])
        p_s[...] = p.astype(BF16)
        al_s[...] = alpha
        m_s[...] = m_new

    def scores(u):
        tile = lax.div(u, nblk)
        kb = lax.rem(u, nblk)
        qbd = jnp.concatenate([q_ref[0, 0, qoff + tile * qsub + c] for c in range(qsub)], axis=1)
        k = k_ref[0, pl.ds(pl.multiple_of(kb * tk, tk), tk), :]
        s = jnp.dot(k, qbd, preferred_element_type=F32)
        s_s[...] = s
        mb_s[...] = jnp.max(s, axis=0, keepdims=True)

    def finalize(tile):
        inv_l = 1.0 / lfin_s[...]
        acc = acc_s[lax.rem(tile, 2)]
        for c in range(qsub):
            a1, a2 = c * sw, c * sw + half
            o_ref[0, 0, tile * qsub + c] = (acc[:, a1:a2] * inv_l[:, a1:a2]
                                            - lam * (acc[:, a2:a2 + half] * inv_l[:, a2:a2 + half]))

    scores(0)
    scale(0)
    scores(1)

    def body(i, carry):
        u = unroll * i + 1
        for d in range(unroll):
            values(u - 1 + d)
            scale(u + d)
            scores(u + 1 + d)
        for d in range(unroll):
            w = u - 1 + d

            @pl.when(lax.rem(w, nblk) == nblk - 1)
            def _():
                finalize(lax.div(w, nblk))
        return carry

    assert (nstep - 2) % unroll == 0 and nblk >= unroll
    lax.fori_loop(0, (nstep - 2) // unroll, body, 0)
    values(nstep - 2)
    scale(nstep - 1)
    values(nstep - 1)
    finalize(ntile - 1)


def _attn_unroll(loop_steps, nblk):
    return max(u for u in range(1, min(ATT_MAX_UNROLL, nblk) + 1) if loop_steps % u == 0)


def _attention(qat, ka, vat, lamv, t, lambda_init):
    bsz, _, nsub, _, sw = qat.shape
    n = ka.shape[1]
    tq, tk = ATT_TQ, ATT_TK
    sub = sw // 2
    nblk = n // tk
    kern = functools.partial(_attn_kernel, tq=tq, tk=tk, nblk=nblk, ntile=t // tq, qsub=tq // sub,
                             qoff=(n - t) // sub, unroll=_attn_unroll((t // tq) * nblk - 2, nblk), lambda_init=lambda_init)
    row = lambda: pltpu.VMEM((1, 2 * tq), F32)
    return pl.pallas_call(
        kern,
        grid=(bsz, DA_HEADS),
        in_specs=[
            pl.BlockSpec((1, 1, nsub, HEAD_LANES, sw), lambda b, h: (b, h, 0, 0, 0)),
            pl.BlockSpec((1, n, HEAD_LANES), lambda b, h: (b, 0, h)),
            pl.BlockSpec((1, nblk, DA_V, tk), lambda b, h: (b, 0, h, 0)),
            pl.BlockSpec(lamv.shape, lambda b, h: (0, 0)),
        ],
        out_specs=pl.BlockSpec((1, 1, t // sub, DA_V, sub), lambda b, h: (b, h, 0, 0, 0)),
        out_shape=jax.ShapeDtypeStruct((bsz, DA_HEADS, t // sub, DA_V, sub), F32),
        scratch_shapes=[row(), row(), row(), pltpu.VMEM((2, DA_V, 2 * tq), F32), pltpu.VMEM((tk, 2 * tq), F32),
                        row(), row(), pltpu.VMEM((tk, 2 * tq), BF16)],
        compiler_params=pltpu.CompilerParams(
            dimension_semantics=("arbitrary", "arbitrary"), vmem_limit_bytes=VMEM_LIMIT_BYTES),
        name="diff_attn",
    )(qat, ka, vat, lamv)


def _mlstm_kernel(qf_ref, kf_ref, vf_ref, gf_ref, cf_ref, qb_ref, kb_ref, vb_ref, gb_ref, cb_ref,
                  hf_ref, hb_ref, c_s, m_s, *, L):
    j = pl.program_id(1)

    @pl.when(j == 0)
    def _():
        c_s[...] = jnp.zeros(c_s.shape, F32)
        m_s[...] = jnp.zeros(m_s.shape, F32)

    sidx = lax.broadcasted_iota(jnp.int32, (L, L), 0)
    tidx = lax.broadcasted_iota(jnp.int32, (L, L), 1)
    row128 = lax.broadcasted_iota(jnp.int32, (LANES, L), 0)
    lane128 = lax.broadcasted_iota(jnp.int32, (ML_STATE_ROWS, LANES), 1)
    row16 = lax.broadcasted_iota(jnp.int32, (16, L), 0)

    chains = range(2 * ML_HEADS)
    rev = [r >= ML_HEADS for r in chains]
    head = [r % ML_HEADS for r in chains]
    refs = [(qb_ref, kb_ref, vb_ref, gb_ref, cb_ref) if rev[r] else (qf_ref, kf_ref, vf_ref, gf_ref, cf_ref)
            for r in chains]
    pairs = [slice((head[r] // 2) * LANES, (head[r] // 2 + 1) * LANES) for r in chains]
    odd = [head[r] % 2 == 1 for r in chains]

    qpair = [refs[r][0][0, pairs[r], :] for r in chains]
    kpair = [refs[r][1][0, :, pairs[r]] for r in chains]
    vt = [refs[r][2][0, head[r] * ML_V:(head[r] + 1) * ML_V, :] for r in chains]
    b_row = [refs[r][3][0, r:r + 1, :] for r in chains]
    c_row = [refs[r][3][0, 8 + r:9 + r, :] for r in chains]
    cm_row = [refs[r][3][0, 16 + r:17 + r, :] for r in chains]
    c_col = [refs[r][4][0, :, r:r + 1] for r in chains]
    m_prev = [m_s[r] for r in chains]
    state = [c_s[r] for r in chains]
    m_run = [jnp.maximum(m_prev[r], cm_row[r]) for r in chains]

    qt = [jnp.where((row128 >= ML_QK) if odd[r] else (row128 < ML_QK), qpair[r], jnp.zeros_like(qpair[r]))
          for r in chains]
    st = [jnp.dot(kpair[r], qt[r], preferred_element_type=F32) for r in chains]
    cq = [jnp.dot(state[r].astype(BF16), qpair[r], preferred_element_type=F32) for r in chains]
    sp = [jnp.where((sidx >= tidx) if rev[r] else (sidx <= tidx),
                    st[r] * jnp.exp2(c_col[r] - m_run[r]), 0.0) for r in chains]
    den_intra = [jnp.sum(sp[r], axis=0, keepdims=True) for r in chains]
    num_intra = [jnp.dot(vt[r], sp[r].astype(BF16), preferred_element_type=F32) for r in chains]

    b_tot = [b_row[r][:, 0:1] if rev[r] else b_row[r][:, L - 1:L] for r in chains]
    cm_end = [cm_row[r][:, 0:1] if rev[r] else cm_row[r][:, L - 1:L] for r in chains]
    m_end = [jnp.maximum(m_prev[r][:, 0:1], cm_end[r]) for r in chains]
    w_row = [jnp.exp2(c_row[r] - m_end[r]) for r in chains]
    aug = [jnp.concatenate([vt[r].astype(F32) * w_row[r],
                            jnp.where(row16 == 0, jnp.broadcast_to(w_row[r], (16, L)), 0.0)], axis=0)
           for r in chains]
    upd = [jnp.dot(aug[r].astype(BF16), kpair[r], preferred_element_type=F32) for r in chains]

    for r in chains:
        w_inter = jnp.exp2(m_prev[r] - m_run[r])
        num = num_intra[r] + w_inter * cq[r][0:ML_V]
        den = den_intra[r] + w_inter * cq[r][ML_V:ML_V + 1]
        ht = num * (1.0 / jnp.maximum(jnp.abs(den), jnp.exp2(-(b_row[r] + m_run[r]))))
        out_ref = hb_ref if rev[r] else hf_ref
        out_ref[0, :, head[r] * ML_V:(head[r] + 1) * ML_V] = ht.T
        own_lanes = (lane128 >= ML_QK) if odd[r] else (lane128 < ML_QK)
        c_s[r] = jnp.exp2(m_prev[r][:, 0:1] - m_end[r]) * state[r] + jnp.where(own_lanes, upd[r], 0.0)
        m_s[r] = jnp.broadcast_to(b_tot[r] + m_end[r], (1, L))


def _mlstm(qmt, km, vmt, gt, cc, t):
    bsz, _, n = qmt.shape
    L = ML_CHUNK_LEN
    nch = n // L
    nctx_ch = (n - t) // L
    nlat = t // L

    def fwd_c(j):
        return j

    def bwd_c(j):
        return jnp.where(j < nctx_ch, nctx_ch - 1 - j, nch - 1 - (j - nctx_ch))

    def specs(cfn):
        return [
            pl.BlockSpec((1, ML_QK_COLS, L), lambda b, j: (b, 0, cfn(j))),
            pl.BlockSpec((1, L, ML_QK_COLS), lambda b, j: (b, cfn(j), 0)),
            pl.BlockSpec((1, ML_V_COLS, L), lambda b, j: (b, 0, cfn(j))),
            pl.BlockSpec((1, GATE_ROWS, L), lambda b, j: (b, 0, cfn(j))),
            pl.BlockSpec((1, L, LANES), lambda b, j: (b, cfn(j), 0)),
        ]

    lat = lambda j: jnp.maximum(j - nctx_ch, 0)
    out_specs = [
        pl.BlockSpec((1, L, ML_V_COLS), lambda b, j: (b, lat(j), 0)),
        pl.BlockSpec((1, L, ML_V_COLS), lambda b, j: (b, nlat - 1 - lat(j), 0)),
    ]
    out_shape = [jax.ShapeDtypeStruct((bsz, t, ML_V_COLS), F32)] * 2
    return pl.pallas_call(
        functools.partial(_mlstm_kernel, L=L),
        grid=(bsz, nch),
        in_specs=specs(fwd_c) + specs(bwd_c),
        out_specs=out_specs,
        out_shape=out_shape,
        scratch_shapes=[pltpu.VMEM((2 * ML_HEADS, ML_STATE_ROWS, LANES), F32),
                        pltpu.VMEM((2 * ML_HEADS, 1, L), F32)],
        compiler_params=pltpu.CompilerParams(
            dimension_semantics=("arbitrary", "arbitrary"), vmem_limit_bytes=VMEM_LIMIT_BYTES),
        name="mlstm",
    )(qmt, km, vmt, gt, cc, qmt, km, vmt, gt, cc)


def _final_kernel(x_ref, da_ref, hf_ref, hb_ref, mo_ref, g1_ref, sh2_ref, sc2_ref, g2_ref,
                  sub_g_ref, mlg_ref, n2g_ref, fg_ref, wo_ref, w1_ref, w2_ref, o_ref, *, lambda_init):
    x = x_ref[0]
    da_rows = []
    for c in range(da_ref.shape[2]):
        heads = []
        for h in range(DA_HEADS):
            o = da_ref[0, h, c]
            y = o * lax.rsqrt(jnp.mean(o * o, axis=0, keepdims=True) + EPS) * sub_g_ref[...] * (1.0 - lambda_init)
            heads.append(y.T.astype(BF16))
        da_rows.append(jnp.concatenate(heads, axis=1))
    da = jnp.concatenate(da_rows, axis=0)
    hsum = hf_ref[0] + hb_ref[0]
    parts = []
    for h in range(ML_HEADS):
        u = hsum[:, h * ML_V:(h + 1) * ML_V]
        parts.append(u * lax.rsqrt(jnp.mean(u * u, axis=1, keepdims=True) + EPS))
    ml = jnp.concatenate(parts, axis=1) * mlg_ref[...] * mo_ref[0].astype(F32)
    cat = jnp.concatenate([da, ml.astype(BF16)], axis=1)
    y = jnp.dot(cat, wo_ref[...], preferred_element_type=F32)
    x1 = x + g1_ref[0] * y
    xn = (_rms(x1, n2g_ref[...]) * (1.0 + sc2_ref[0]) + sh2_ref[0]).astype(BF16)
    hid = jnp.dot(xn, w1_ref[...], preferred_element_type=F32)
    hid = jnp.square(jnp.maximum(hid, 0.0)).astype(BF16)
    x2 = x1 + g2_ref[0] * jnp.dot(hid, w2_ref[...], preferred_element_type=F32)
    o_ref[0] = _rms(x2, fg_ref[...])


def _final(x, da, hf, hb, mo, g1, sh2, sc2, g2, subg, mlg, n2g, fg, wo, w1, w2, lambda_init):
    bsz, t, d = x.shape
    tm = FINAL_TILE
    sub = da.shape[-1]
    tok = lambda w: pl.BlockSpec((1, tm, w), lambda b, i: (b, i, 0))
    mod = pl.BlockSpec((1, 1, d), lambda b, i: (b, 0, 0))
    const = lambda a: pl.BlockSpec(a.shape, lambda b, i: (0, 0), pipeline_mode=pl.Buffered(1))
    return pl.pallas_call(
        functools.partial(_final_kernel, lambda_init=lambda_init),
        grid=(bsz, t // tm),
        in_specs=[tok(d), pl.BlockSpec((1, DA_HEADS, tm // sub, DA_V, sub), lambda b, i: (b, 0, i, 0, 0)),
                  tok(ML_V_COLS), tok(ML_V_COLS), tok(ML_V_COLS),
                  mod, mod, mod, mod, const(subg), const(mlg), const(n2g), const(fg), const(wo), const(w1), const(w2)],
        out_specs=tok(d),
        out_shape=jax.ShapeDtypeStruct((bsz, t, d), F32),
        compiler_params=pltpu.CompilerParams(
            dimension_semantics=("arbitrary", "arbitrary"), vmem_limit_bytes=VMEM_LIMIT_BYTES),
        name="final",
    )(x, da, hf, hb, mo, g1, sh2, sc2, g2, subg, mlg, n2g, fg, wo, w1, w2)


def _rope_tables(t, nctx):
    rows = t // GRID_W
    half = DA_QK // 2
    inv = ROPE_BASE ** (-jnp.arange(0, half, 2, dtype=F32) / half)
    ar = jnp.arange(rows, dtype=F32)[:, None] * inv
    ac = jnp.arange(GRID_W, dtype=F32)[:, None] * inv
    sign = jnp.where(jnp.arange(DA_QK) % 32 < 16, -1.0, 1.0).astype(F32)

    def table(fn, ctx_row):
        r = jnp.repeat(fn(ar), GRID_W, axis=0)
        c = jnp.tile(fn(ac), (rows, 1))
        lat = jnp.concatenate([r, r, c, c], axis=-1)
        return jnp.concatenate([jnp.broadcast_to(ctx_row, (nctx, DA_QK)), lat], axis=0)

    cos = table(jnp.cos, jnp.ones((DA_QK,), F32))
    sin = table(jnp.sin, jnp.zeros((DA_QK,), F32)) * sign
    return jnp.tile(cos, (1, 2)), jnp.tile(sin, (1, 2))


def _split_weights(w_in_l, b_gate_l):
    nq = DA_HEADS * 2 * DA_QK
    o = 0
    wdq = w_in_l[:, o:o + nq]; o += nq
    wdk = w_in_l[:, o:o + nq]; o += nq
    wdv = w_in_l[:, o:o + DA_HEADS * DA_V]; o += DA_HEADS * DA_V
    wmq = w_in_l[:, o:o + ML_HEADS * ML_QK]; o += ML_HEADS * ML_QK
    wmk = w_in_l[:, o:o + ML_HEADS * ML_QK]; o += ML_HEADS * ML_QK
    wmv = w_in_l[:, o:o + ML_HEADS * ML_V]; o += ML_HEADS * ML_V
    wmo = w_in_l[:, o:o + ML_HEADS * ML_V]; o += ML_HEADS * ML_V
    wmg = w_in_l[:, o:o + 4 * ML_HEADS]
    perm = jnp.array([0, 1, 2, 3, 8, 9, 10, 11, 4, 5, 6, 7, 12, 13, 14, 15])
    wmg = wmg[:, perm]
    bg = b_gate_l[perm].reshape(16, 1).astype(F32)
    qscale = DA_QK ** -0.5 * LOG2_E
    kscale = ML_QK ** -0.5
    wn = jnp.concatenate([wdk, wmk * kscale, wmo], axis=1).astype(BF16)
    wt = jnp.concatenate([wdq * qscale, wdv, wmq, wmv, wmg], axis=1).T.astype(BF16)
    return wn, wt, bg


def kernel(x, c, ctx, c_ctx, w_ada, b_ada, norm1_g, norm2_g, w_in, b_gate, lam_q1, lam_k1, lam_q2, lam_k2,
           subln_g, mlstm_norm_g, w_out, w_fc1, w_fc2, final_g):
    bsz, t, d = x.shape
    nctx = ctx.shape[1]
    depth = w_ada.shape[0]
    assert depth == 1, "single-layer block: the context stream is never updated"
    assert TOK_TILE == ML_CHUNK_LEN, "the projection computes per-chunk gate sums on its own tile"
    assert nctx % TOK_TILE == 0 and ATT_TQ % TOK_TILE == 0 and ATT_TK % TOK_TILE == 0
    assert t % ATT_TQ == 0 and t % ML_CHUNK_LEN == 0 and (nctx + t) % ATT_TK == 0 and t % GRID_W == 0
    assert t % FINAL_TILE == 0
    assert (t // ATT_TQ) * ((nctx + t) // ATT_TK) >= 3, "the attention pipeline needs at least three steps"
    lambda_init = 0.2

    cc = jnp.concatenate([c, c_ctx[None, :], jnp.zeros((8 - bsz - 1, d), F32)], axis=0)
    mod = _adaln(cc, w_ada[0], b_ada[0][None, :])
    mb = mod[:bsz].reshape(bsz, 1, 6, d)
    sh1, sc1, g1, sh2, sc2, g2 = [mb[:, :, k] for k in range(6)]
    mc = mod[bsz:bsz + 1].reshape(1, 1, 6, d)
    csh1, csc1 = mc[:, :, 0], mc[:, :, 1]

    wn, wt, bg = _split_weights(w_in[0], b_gate[0])
    cosn, sinn = _rope_tables(t, nctx)
    ka, km, mo, qat, vat, qmt, vmt, gt, cc = _project(
        x, ctx, sc1, sh1, csc1, csh1, norm1_g[0][None, :], wn, wt, bg, cosn, sinn)

    lamv = jnp.stack([lam_q1[0], lam_k1[0], lam_q2[0], lam_k2[0]]).astype(F32)
    da = _attention(qat, ka, vat, lamv, t, lambda_init)
    hf, hb = _mlstm(qmt, km, vmt, gt, cc, t)

    return _final(x, da, hf, hb, mo, g1, sh2, sc2, g2, subln_g[0].reshape(DA_V, 1), mlstm_norm_g[0][None, :],
                  norm2_g[0][None, :], final_g[None, :], w_out[0].astype(BF16), w_fc1[0].astype(BF16),
                  w_fc2[0].astype(BF16), lambda_init)
```

```python
import functools

import jax
import jax.numpy as jnp
from jax import lax
from jax.experimental import pallas as pl
from jax.experimental.pallas import tpu as pltpu

DA_HEADS = 4
DA_QK = 64
DA_V = 128
ML_HEADS = 4
ML_QK = 64
ML_V = 128
GRID_W = 64
N_GATES = 4 * ML_HEADS
GATE_ROWS = 3 * 2 * ML_HEADS
ROPE_BASE = 10000.0
EPS = 1e-6
LOG2_E = 1.4426950408889634

LANES = 128
DA_K_COLS = DA_HEADS * 2 * DA_QK
DA_V_COLS = DA_HEADS * DA_V
ML_QK_COLS = ML_HEADS * ML_QK
ML_V_COLS = ML_HEADS * ML_V
HEAD_LANES = 2 * DA_QK
assert HEAD_LANES == LANES and DA_V == LANES and ML_V == LANES and 2 * ML_QK == LANES

TOK_TILE = 256
FINAL_TILE = 512
ATT_TQ = 512
ATT_TK = 768
ATT_MAX_UNROLL = 2
ML_CHUNK_LEN = 256
ML_STATE_ROWS = 144
VMEM_LIMIT_BYTES = 56 * 1024 * 1024

F32 = jnp.float32
BF16 = jnp.bfloat16


def _rms(x, g):
    return x * lax.rsqrt(jnp.mean(x * x, axis=-1, keepdims=True) + EPS) * g


def _adaln_kernel(c_ref, w_ref, b_ref, o_ref):
    c = c_ref[...]
    s = (c * jax.nn.sigmoid(c)).astype(BF16)
    o_ref[...] = jnp.dot(s, w_ref[...].astype(BF16), preferred_element_type=F32) + b_ref[...]


def _adaln(cc, w, b):
    rows, d = cc.shape
    n = w.shape[1]
    bn = d
    return pl.pallas_call(
        _adaln_kernel,
        grid=(n // bn,),
        in_specs=[pl.BlockSpec((rows, d), lambda j: (0, 0)),
                  pl.BlockSpec((d, bn), lambda j: (0, j)),
                  pl.BlockSpec((1, bn), lambda j: (0, j))],
        out_specs=pl.BlockSpec((rows, bn), lambda j: (0, j)),
        out_shape=jax.ShapeDtypeStruct((rows, n), F32),
        name="adaln",
    )(cc, w, b)


def _proj_kernel(x_ref, ctx_ref, sc_ref, sh_ref, csc_ref, csh_ref, g_ref, wn_ref, wt_ref, bg_ref,
                 cosn_ref, sinn_ref,
                 ka_ref, km_ref, mo_ref, qat_ref, vat_ref, qmt_ref, vmt_ref, gt_ref, cc_ref, *, nct):
    i = pl.program_id(0)
    is_ctx = i < nct
    x = jnp.where(is_ctx, ctx_ref[0], x_ref[0])
    sc = jnp.where(is_ctx, csc_ref[0], sc_ref[0])
    sh = jnp.where(is_ctx, csh_ref[0], sh_ref[0])
    xm = _rms(x, g_ref[...]) * (1.0 + sc) + sh
    xb = xm.astype(BF16)
    nt = (((1,), (1,)), ((), ()))
    ngate = DA_K_COLS + DA_V_COLS + ML_QK_COLS + ML_V_COLS

    g = lax.dot_general(wt_ref[ngate:ngate + N_GATES, :], xb, nt, preferred_element_type=F32) + bg_ref[...]
    li8 = g[0:8]
    lf8 = jnp.minimum(g[8:16], 0.0) - jnp.log1p(jnp.exp(-jnp.abs(g[8:16])))
    tm = g.shape[1]
    row8 = lax.broadcasted_iota(jnp.int32, (8, tm), 0)
    lane8 = lax.broadcasted_iota(jnp.int32, (8, tm), 1)
    is_f = row8 < ML_HEADS

    def scan(x, op, fill):
        step = 1
        while step < tm:
            prev = jnp.where(lane8 >= step, pltpu.roll(x, step, axis=1), fill)
            nxt = jnp.where(lane8 < tm - step, pltpu.roll(x, tm - step, axis=1), fill)
            x = op(x, jnp.where(is_f, prev, nxt))
            step *= 2
        return x

    b8 = scan(lf8, jnp.add, 0.0)
    c8 = li8 - b8
    cm8 = scan(c8, jnp.maximum, -jnp.inf)
    gt_ref[0] = jnp.concatenate([b8, c8, cm8], axis=0) * LOG2_E
    cc_ref[0] = (jnp.concatenate([c8, jnp.zeros((LANES - 8, tm), F32)], axis=0) * LOG2_E).T

    hn = jnp.dot(xb, wn_ref[...], preferred_element_type=F32)
    ht = lax.dot_general(wt_ref[0:ngate, :], xb, nt, preferred_element_type=F32)

    cosn, sinn = cosn_ref[...], sinn_ref[...]
    cost, sint = cosn.T, sinn.T
    lane = lax.broadcasted_iota(jnp.int32, (tm, HEAD_LANES), 1)
    low_half = (lane % 32) < 16
    for h in range(DA_HEADS):
        lo, hi = h * HEAD_LANES, (h + 1) * HEAD_LANES
        k = hn[:, lo:hi]
        kswap = jnp.where(low_half, pltpu.roll(k, HEAD_LANES - 16, axis=1), pltpu.roll(k, 16, axis=1))
        ka_ref[0, :, lo:hi] = (k * cosn + kswap * sinn).astype(BF16)
        q = ht[lo:hi]
        qswap = jnp.concatenate([q[r0 + 16:r0 + 32] if part == 0 else q[r0:r0 + 16]
                                 for r0 in range(0, HEAD_LANES, 32) for part in (0, 1)], axis=0)
        q = (q * cost + qswap * sint).astype(BF16)
        qrow = lax.broadcasted_iota(jnp.int32, q.shape, 0)
        zero = jnp.zeros_like(q)
        qat_ref[0, h, 0] = jnp.concatenate(
            [jnp.where(qrow < DA_QK, q, zero), jnp.where(qrow >= DA_QK, q, zero)], axis=1)
    km_ref[0] = hn[:, DA_K_COLS:DA_K_COLS + ML_QK_COLS].astype(BF16)
    mo_ref[0] = jax.nn.sigmoid(hn[:, DA_K_COLS + ML_QK_COLS:DA_K_COLS + ML_QK_COLS + ML_V_COLS]).astype(BF16)

    o = DA_K_COLS
    vat_ref[0, 0] = ht[o:o + DA_V_COLS].astype(BF16)
    o += DA_V_COLS
    qmt_ref[0] = ht[o:o + ML_QK_COLS].astype(BF16)
    o += ML_QK_COLS
    vmt_ref[0] = ht[o:o + ML_V_COLS].astype(BF16)


def _project(x, ctx, sc, sh, csc, csh, g, wn, wt, bg, cosn, sinn):
    bsz, t, d = x.shape
    nctx = ctx.shape[1]
    n = nctx + t
    tm = TOK_TILE
    nct = nctx // tm
    r = ATT_TK // tm
    const2 = lambda i, b: (0, 0)
    const3 = lambda i, b: (0, 0, 0)
    in_specs = [
        pl.BlockSpec((1, tm, d), lambda i, b: (b, jnp.maximum(i - nct, 0), 0)),
        pl.BlockSpec((1, tm, d), lambda i, b: (b, jnp.minimum(i, nct - 1), 0)),
        pl.BlockSpec((1, 1, d), lambda i, b: (b, 0, 0)),
        pl.BlockSpec((1, 1, d), lambda i, b: (b, 0, 0)),
        pl.BlockSpec((1, 1, d), const3),
        pl.BlockSpec((1, 1, d), const3),
        pl.BlockSpec((1, d), const2),
        pl.BlockSpec(wn.shape, const2),
        pl.BlockSpec(wt.shape, const2),
        pl.BlockSpec(bg.shape, const2),
        pl.BlockSpec((tm, HEAD_LANES), lambda i, b: (i, 0)),
        pl.BlockSpec((tm, HEAD_LANES), lambda i, b: (i, 0)),
    ]
    out_specs = [
        pl.BlockSpec((1, tm, DA_K_COLS), lambda i, b: (b, i, 0)),
        pl.BlockSpec((1, tm, ML_QK_COLS), lambda i, b: (b, i, 0)),
        pl.BlockSpec((1, tm, ML_V_COLS), lambda i, b: (b, jnp.where(i < nct, n // tm - nct + i, i - nct), 0)),
        pl.BlockSpec((1, DA_HEADS, 1, HEAD_LANES, 2 * tm), lambda i, b: (b, 0, i, 0, 0)),
        pl.BlockSpec((1, 1, DA_V_COLS, tm), lambda i, b: (b, i // r, 0, i % r)),
        pl.BlockSpec((1, ML_QK_COLS, tm), lambda i, b: (b, 0, i)),
        pl.BlockSpec((1, ML_V_COLS, tm), lambda i, b: (b, 0, i)),
        pl.BlockSpec((1, GATE_ROWS, tm), lambda i, b: (b, 0, i)),
        pl.BlockSpec((1, tm, LANES), lambda i, b: (b, i, 0)),
    ]
    out_shape = [
        jax.ShapeDtypeStruct((bsz, n, DA_K_COLS), BF16),
        jax.ShapeDtypeStruct((bsz, n, ML_QK_COLS), BF16),
        jax.ShapeDtypeStruct((bsz, n, ML_V_COLS), BF16),
        jax.ShapeDtypeStruct((bsz, DA_HEADS, n // tm, HEAD_LANES, 2 * tm), BF16),
        jax.ShapeDtypeStruct((bsz, n // ATT_TK, DA_V_COLS, ATT_TK), BF16),
        jax.ShapeDtypeStruct((bsz, ML_QK_COLS, n), BF16),
        jax.ShapeDtypeStruct((bsz, ML_V_COLS, n), BF16),
        jax.ShapeDtypeStruct((bsz, GATE_ROWS, n), F32),
        jax.ShapeDtypeStruct((bsz, n, LANES), F32),
    ]
    return pl.pallas_call(
        functools.partial(_proj_kernel, nct=nct),
        grid=(n // tm, bsz),
        in_specs=in_specs,
        out_specs=out_specs,
        out_shape=out_shape,
        compiler_params=pltpu.CompilerParams(
            dimension_semantics=("arbitrary", "arbitrary"), vmem_limit_bytes=VMEM_LIMIT_BYTES),
        name="project",
    )(x, ctx, sc, sh, csc, csh, g, wn, wt, bg, cosn, sinn)


def _attn_kernel(q_ref, k_ref, v_ref, lam_ref, o_ref, m_s, l_s, lfin_s, acc_s, s_s, mb_s, al_s, p_s, *,
                 tq, tk, nblk, ntile, qsub, qoff, unroll, lambda_init):
    nstep = ntile * nblk
    sw = q_ref.shape[-1]
    half = sw // 2

    m_s[...] = jnp.zeros(m_s.shape, F32)
    l_s[...] = jnp.zeros(l_s.shape, F32)
    lfin_s[...] = jnp.ones(lfin_s.shape, F32)
    acc_s[...] = jnp.zeros(acc_s.shape, F32)

    lv = lam_ref[...]
    lam = (jnp.exp(jnp.sum(lv[0:1] * lv[1:2], axis=1, keepdims=True))
           - jnp.exp(jnp.sum(lv[2:3] * lv[3:4], axis=1, keepdims=True)) + lambda_init)

    def values(u):
        kb = lax.rem(u, nblk)
        slot = lax.rem(lax.div(u, nblk), 2)
        pv = jnp.dot(v_ref[0, kb], p_s[...], preferred_element_type=F32)
        acc_s[slot] = al_s[...] * acc_s[slot] + pv

    def scale(u):
        kb = lax.rem(u, nblk)
        first = kb == 0
        m_old = jnp.where(first, -jnp.inf, m_s[...])
        l_old = jnp.where(first, 0.0, l_s[...])
        m_new = jnp.maximum(m_old, mb_s[...])
        alpha = jnp.exp2(m_old - m_new)
        p = jnp.exp2(s_s[...] - m_new)
        l_new = alpha * l_old + jnp.sum(p, axis=0, keepdims=True)
        l_s[...] = l_new
        lfin_s[...] = jnp.where(kb == nblk - 1, l_new, lfin_s[...])
        p_s[...] = p.astype(BF16)
        al_s[...] = alpha
        m_s[...] = m_new

    def scores(u):
        tile = lax.div(u, nblk)
        kb = lax.rem(u, nblk)
        qbd = jnp.concatenate([q_ref[0, 0, qoff + tile * qsub + c] for c in range(qsub)], axis=1)
        k = k_ref[0, pl.ds(pl.multiple_of(kb * tk, tk), tk), :]
        s = jnp.dot(k, qbd, preferred_element_type=F32)
        s_s[...] = s
        mb_s[...] = jnp.max(s, axis=0, keepdims=True)

    def finalize(tile):
        inv_l = 1.0 / lfin_s[...]
        acc = acc_s[lax.rem(tile, 2)]
        for c in range(qsub):
            a1, a2 = c * sw, c * sw + half
            o_ref[0, 0, tile * qsub + c] = (acc[:, a1:a2] * inv_l[:, a1:a2]
                                            - lam * (acc[:, a2:a2 + half] * inv_l[:, a2:a2 + half]))

    scores(0)
    scale(0)
    scores(1)

    def body(i, carry):
        u = unroll * i + 1
        for d in range(unroll):
            values(u - 1 + d)
            scale(u + d)
            scores(u + 1 + d)
        for d in range(unroll):
            w = u - 1 + d

            @pl.when(lax.rem(w, nblk) == nblk - 1)
            def _():
                finalize(lax.div(w, nblk))
        return carry

    assert (nstep - 2) % unroll == 0 and nblk >= unroll
    lax.fori_loop(0, (nstep - 2) // unroll, body, 0)
    values(nstep - 2)
    scale(nstep - 1)
    values(nstep - 1)
    finalize(ntile - 1)


def _attn_unroll(loop_steps, nblk):
    return max(u for u in range(1, min(ATT_MAX_UNROLL, nblk) + 1) if loop_steps % u == 0)


def _attention(qat, ka, vat, lamv, t, lambda_init):
    bsz, _, nsub, _, sw = qat.shape
    n = ka.shape[1]
    tq, tk = ATT_TQ, ATT_TK
    sub = sw // 2
    nblk = n // tk
    kern = functools.partial(_attn_kernel, tq=tq, tk=tk, nblk=nblk, ntile=t // tq, qsub=tq // sub,
                             qoff=(n - t) // sub, unroll=_attn_unroll((t // tq) * nblk - 2, nblk), lambda_init=lambda_init)
    row = lambda: pltpu.VMEM((1, 2 * tq), F32)
    return pl.pallas_call(
        kern,
        grid=(bsz, DA_HEADS),
        in_specs=[
            pl.BlockSpec((1, 1, nsub, HEAD_LANES, sw), lambda b, h: (b, h, 0, 0, 0)),
            pl.BlockSpec((1, n, HEAD_LANES), lambda b, h: (b, 0, h)),
            pl.BlockSpec((1, nblk, DA_V, tk), lambda b, h: (b, 0, h, 0)),
            pl.BlockSpec(lamv.shape, lambda b, h: (0, 0)),
        ],
        out_specs=pl.BlockSpec((1, 1, t // sub, DA_V, sub), lambda b, h: (b, h, 0, 0, 0)),
        out_shape=jax.ShapeDtypeStruct((bsz, DA_HEADS, t // sub, DA_V, sub), F32),
        scratch_shapes=[row(), row(), row(), pltpu.VMEM((2, DA_V, 2 * tq), F32), pltpu.VMEM((tk, 2 * tq), F32),
                        row(), row(), pltpu.VMEM((tk, 2 * tq), BF16)],
        compiler_params=pltpu.CompilerParams(
            dimension_semantics=("arbitrary", "arbitrary"), vmem_limit_bytes=VMEM_LIMIT_BYTES),
        name="diff_attn",
    )(qat, ka, vat, lamv)


def _mlstm_kernel(qf_ref, kf_ref, vf_ref, gf_ref, cf_ref, qb_ref, kb_ref, vb_ref, gb_ref, cb_ref,
                  hf_ref, hb_ref, c_s, m_s, *, L):
    j = pl.program_id(1)

    @pl.when(j == 0)
    def _():
        c_s[...] = jnp.zeros(c_s.shape, F32)
        m_s[...] = jnp.zeros(m_s.shape, F32)

    sidx = lax.broadcasted_iota(jnp.int32, (L, L), 0)
    tidx = lax.broadcasted_iota(jnp.int32, (L, L), 1)
    row128 = lax.broadcasted_iota(jnp.int32, (LANES, L), 0)
    lane128 = lax.broadcasted_iota(jnp.int32, (ML_STATE_ROWS, LANES), 1)
    row16 = lax.broadcasted_iota(jnp.int32, (16, L), 0)

    chains = range(2 * ML_HEADS)
    rev = [r >= ML_HEADS for r in chains]
    head = [r % ML_HEADS for r in chains]
    refs = [(qb_ref, kb_ref, vb_ref, gb_ref, cb_ref) if rev[r] else (qf_ref, kf_ref, vf_ref, gf_ref, cf_ref)
            for r in chains]
    pairs = [slice((head[r] // 2) * LANES, (head[r] // 2 + 1) * LANES) for r in chains]
    odd = [head[r] % 2 == 1 for r in chains]

    qpair = [refs[r][0][0, pairs[r], :] for r in chains]
    kpair = [refs[r][1][0, :, pairs[r]] for r in chains]
    vt = [refs[r][2][0, head[r] * ML_V:(head[r] + 1) * ML_V, :] for r in chains]
    b_row = [refs[r][3][0, r:r + 1, :] for r in chains]
    c_row = [refs[r][3][0, 8 + r:9 + r, :] for r in chains]
    cm_row = [refs[r][3][0, 16 + r:17 + r, :] for r in chains]
    c_col = [refs[r][4][0, :, r:r + 1] for r in chains]
    m_prev = [m_s[r] for r in chains]
    state = [c_s[r] for r in chains]
    m_run = [jnp.maximum(m_prev[r], cm_row[r]) for r in chains]

    qt = [jnp.where((row128 >= ML_QK) if odd[r] else (row128 < ML_QK), qpair[r], jnp.zeros_like(qpair[r]))
          for r in chains]
    st = [jnp.dot(kpair[r], qt[r], preferred_element_type=F32) for r in chains]
    cq = [jnp.dot(state[r].astype(BF16), qpair[r], preferred_element_type=F32) for r in chains]
    sp = [jnp.where((sidx >= tidx) if rev[r] else (sidx <= tidx),
                    st[r] * jnp.exp2(c_col[r] - m_run[r]), 0.0) for r in chains]
    den_intra = [jnp.sum(sp[r], axis=0, keepdims=True) for r in chains]
    num_intra = [jnp.dot(vt[r], sp[r].astype(BF16), preferred_element_type=F32) for r in chains]

    b_tot = [b_row[r][:, 0:1] if rev[r] else b_row[r][:, L - 1:L] for r in chains]
    cm_end = [cm_row[r][:, 0:1] if rev[r] else cm_row[r][:, L - 1:L] for r in chains]
    m_end = [jnp.maximum(m_prev[r][:, 0:1], cm_end[r]) for r in chains]
    w_row = [jnp.exp2(c_row[r] - m_end[r]) for r in chains]
    aug = [jnp.concatenate([vt[r].astype(F32) * w_row[r],
                            jnp.where(row16 == 0, jnp.broadcast_to(w_row[r], (16, L)), 0.0)], axis=0)
           for r in chains]
    upd = [jnp.dot(aug[r].astype(BF16), kpair[r], preferred_element_type=F32) for r in chains]

    for r in chains:
        w_inter = jnp.exp2(m_prev[r] - m_run[r])
        num = num_intra[r] + w_inter * cq[r][0:ML_V]
        den = den_intra[r] + w_inter * cq[r][ML_V:ML_V + 1]
        ht = num * (1.0 / jnp.maximum(jnp.abs(den), jnp.exp2(-(b_row[r] + m_run[r]))))
        out_ref = hb_ref if rev[r] else hf_ref
        out_ref[0, 0, head[r] * ML_V:(head[r] + 1) * ML_V, :] = ht
        own_lanes = (lane128 >= ML_QK) if odd[r] else (lane128 < ML_QK)
        c_s[r] = jnp.exp2(m_prev[r][:, 0:1] - m_end[r]) * state[r] + jnp.where(own_lanes, upd[r], 0.0)
        m_s[r] = jnp.broadcast_to(b_tot[r] + m_end[r], (1, L))


def _mlstm(qmt, km, vmt, gt, cc, t):
    bsz, _, n = qmt.shape
    L = ML_CHUNK_LEN
    nch = n // L
    nctx_ch = (n - t) // L
    nlat = t // L

    def fwd_c(j):
        return j

    def bwd_c(j):
        return jnp.where(j < nctx_ch, nctx_ch - 1 - j, nch - 1 - (j - nctx_ch))

    def specs(cfn):
        return [
            pl.BlockSpec((1, ML_QK_COLS, L), lambda b, j: (b, 0, cfn(j))),
            pl.BlockSpec((1, L, ML_QK_COLS), lambda b, j: (b, cfn(j), 0)),
            pl.BlockSpec((1, ML_V_COLS, L), lambda b, j: (b, 0, cfn(j))),
            pl.BlockSpec((1, GATE_ROWS, L), lambda b, j: (b, 0, cfn(j))),
            pl.BlockSpec((1, L, LANES), lambda b, j: (b, cfn(j), 0)),
        ]

    lat = lambda j: jnp.maximum(j - nctx_ch, 0)
    out_specs = [
        pl.BlockSpec((1, 1, ML_V_COLS, L), lambda b, j: (b, lat(j), 0, 0)),
        pl.BlockSpec((1, 1, ML_V_COLS, L), lambda b, j: (b, nlat - 1 - lat(j), 0, 0)),
    ]
    out_shape = [jax.ShapeDtypeStruct((bsz, nlat, ML_V_COLS, L), F32)] * 2
    return pl.pallas_call(
        functools.partial(_mlstm_kernel, L=L),
        grid=(bsz, nch),
        in_specs=specs(fwd_c) + specs(bwd_c),
        out_specs=out_specs,
        out_shape=out_shape,
        scratch_shapes=[pltpu.VMEM((2 * ML_HEADS, ML_STATE_ROWS, LANES), F32),
                        pltpu.VMEM((2 * ML_HEADS, 1, L), F32)],
        compiler_params=pltpu.CompilerParams(
            dimension_semantics=("arbitrary", "arbitrary"), vmem_limit_bytes=VMEM_LIMIT_BYTES),
        name="mlstm",
    )(qmt, km, vmt, gt, cc, qmt, km, vmt, gt, cc)


def _final_kernel(x_ref, da_ref, hf_ref, hb_ref, mo_ref, g1_ref, sh2_ref, sc2_ref, g2_ref,
                  sub_g_ref, mlg_ref, n2g_ref, fg_ref, wo_ref, w1_ref, w2_ref, o_ref, *, lambda_init):
    x = x_ref[0]
    da_rows = []
    for c in range(da_ref.shape[2]):
        heads = []
        for h in range(DA_HEADS):
            o = da_ref[0, h, c]
            y = o * lax.rsqrt(jnp.mean(o * o, axis=0, keepdims=True) + EPS) * sub_g_ref[...] * (1.0 - lambda_init)
            heads.append(y.T.astype(BF16))
        da_rows.append(jnp.concatenate(heads, axis=1))
    da = jnp.concatenate(da_rows, axis=0)
    ml_rows = []
    for c in range(hf_ref.shape[1]):
        hsum = hf_ref[0, c] + hb_ref[0, c]
        parts = []
        for h in range(ML_HEADS):
            u = hsum[h * ML_V:(h + 1) * ML_V]
            parts.append(u * lax.rsqrt(jnp.mean(u * u, axis=0, keepdims=True) + EPS))
        ml_rows.append((jnp.concatenate(parts, axis=0) * mlg_ref[...]).T)
    ml = jnp.concatenate(ml_rows, axis=0) * mo_ref[0].astype(F32)
    cat = jnp.concatenate([da, ml.astype(BF16)], axis=1)
    y = jnp.dot(cat, wo_ref[...], preferred_element_type=F32)
    x1 = x + g1_ref[0] * y
    xn = (_rms(x1, n2g_ref[...]) * (1.0 + sc2_ref[0]) + sh2_ref[0]).astype(BF16)
    hid = jnp.dot(xn, w1_ref[...], preferred_element_type=F32)
    hid = jnp.square(jnp.maximum(hid, 0.0)).astype(BF16)
    x2 = x1 + g2_ref[0] * jnp.dot(hid, w2_ref[...], preferred_element_type=F32)
    o_ref[0] = _rms(x2, fg_ref[...])


def _final(x, da, hf, hb, mo, g1, sh2, sc2, g2, subg, mlg, n2g, fg, wo, w1, w2, lambda_init):
    bsz, t, d = x.shape
    tm = FINAL_TILE
    sub = da.shape[-1]
    chunk = hf.shape[-1]
    hspec = pl.BlockSpec((1, tm // chunk, ML_V_COLS, chunk), lambda b, i: (b, i, 0, 0))
    tok = lambda w: pl.BlockSpec((1, tm, w), lambda b, i: (b, i, 0))
    mod = pl.BlockSpec((1, 1, d), lambda b, i: (b, 0, 0))
    const = lambda a: pl.BlockSpec(a.shape, lambda b, i: (0, 0), pipeline_mode=pl.Buffered(1))
    return pl.pallas_call(
        functools.partial(_final_kernel, lambda_init=lambda_init),
        grid=(bsz, t // tm),
        in_specs=[tok(d), pl.BlockSpec((1, DA_HEADS, tm // sub, DA_V, sub), lambda b, i: (b, 0, i, 0, 0)),
                  hspec, hspec, tok(ML_V_COLS),
                  mod, mod, mod, mod, const(subg), const(mlg), const(n2g), const(fg), const(wo), const(w1), const(w2)],
        out_specs=tok(d),
        out_shape=jax.ShapeDtypeStruct((bsz, t, d), F32),
        compiler_params=pltpu.CompilerParams(
            dimension_semantics=("arbitrary", "arbitrary"), vmem_limit_bytes=VMEM_LIMIT_BYTES),
        name="final",
    )(x, da, hf, hb, mo, g1, sh2, sc2, g2, subg, mlg, n2g, fg, wo, w1, w2)


def _rope_tables(t, nctx):
    rows = t // GRID_W
    half = DA_QK // 2
    inv = ROPE_BASE ** (-jnp.arange(0, half, 2, dtype=F32) / half)
    ar = jnp.arange(rows, dtype=F32)[:, None] * inv
    ac = jnp.arange(GRID_W, dtype=F32)[:, None] * inv
    sign = jnp.where(jnp.arange(DA_QK) % 32 < 16, -1.0, 1.0).astype(F32)

    def table(fn, ctx_row):
        r = jnp.repeat(fn(ar), GRID_W, axis=0)
        c = jnp.tile(fn(ac), (rows, 1))
        lat = jnp.concatenate([r, r, c, c], axis=-1)
        return jnp.concatenate([jnp.broadcast_to(ctx_row, (nctx, DA_QK)), lat], axis=0)

    cos = table(jnp.cos, jnp.ones((DA_QK,), F32))
    sin = table(jnp.sin, jnp.zeros((DA_QK,), F32)) * sign
    return jnp.tile(cos, (1, 2)), jnp.tile(sin, (1, 2))


def _split_weights(w_in_l, b_gate_l):
    nq = DA_HEADS * 2 * DA_QK
    o = 0
    wdq = w_in_l[:, o:o + nq]; o += nq
    wdk = w_in_l[:, o:o + nq]; o += nq
    wdv = w_in_l[:, o:o + DA_HEADS * DA_V]; o += DA_HEADS * DA_V
    wmq = w_in_l[:, o:o + ML_HEADS * ML_QK]; o += ML_HEADS * ML_QK
    wmk = w_in_l[:, o:o + ML_HEADS * ML_QK]; o += ML_HEADS * ML_QK
    wmv = w_in_l[:, o:o + ML_HEADS * ML_V]; o += ML_HEADS * ML_V
    wmo = w_in_l[:, o:o + ML_HEADS * ML_V]; o += ML_HEADS * ML_V
    wmg = w_in_l[:, o:o + 4 * ML_HEADS]
    perm = jnp.array([0, 1, 2, 3, 8, 9, 10, 11, 4, 5, 6, 7, 12, 13, 14, 15])
    wmg = wmg[:, perm]
    bg = b_gate_l[perm].reshape(16, 1).astype(F32)
    qscale = DA_QK ** -0.5 * LOG2_E
    kscale = ML_QK ** -0.5
    wn = jnp.concatenate([wdk, wmk * kscale, wmo], axis=1).astype(BF16)
    wt = jnp.concatenate([wdq * qscale, wdv, wmq, wmv, wmg], axis=1).T.astype(BF16)
    return wn, wt, bg


def kernel(x, c, ctx, c_ctx, w_ada, b_ada, norm1_g, norm2_g, w_in, b_gate, lam_q1, lam_k1, lam_q2, lam_k2,
           subln_g, mlstm_norm_g, w_out, w_fc1, w_fc2, final_g):
    bsz, t, d = x.shape
    nctx = ctx.shape[1]
    depth = w_ada.shape[0]
    assert depth == 1, "single-layer block: the context stream is never updated"
    assert TOK_TILE == ML_CHUNK_LEN, "the projection computes per-chunk gate sums on its own tile"
    assert nctx % TOK_TILE == 0 and ATT_TQ % TOK_TILE == 0 and ATT_TK % TOK_TILE == 0
    assert t % ATT_TQ == 0 and t % ML_CHUNK_LEN == 0 and (nctx + t) % ATT_TK == 0 and t % GRID_W == 0
    assert t % FINAL_TILE == 0
    assert (t // ATT_TQ) * ((nctx + t) // ATT_TK) >= 3, "the attention pipeline needs at least three steps"
    lambda_init = 0.2

    cc = jnp.concatenate([c, c_ctx[None, :], jnp.zeros((8 - bsz - 1, d), F32)], axis=0)
    mod = _adaln(cc, w_ada[0], b_ada[0][None, :])
    mb = mod[:bsz].reshape(bsz, 1, 6, d)
    sh1, sc1, g1, sh2, sc2, g2 = [mb[:, :, k] for k in range(6)]
    mc = mod[bsz:bsz + 1].reshape(1, 1, 6, d)
    csh1, csc1 = mc[:, :, 0], mc[:, :, 1]

    wn, wt, bg = _split_weights(w_in[0], b_gate[0])
    cosn, sinn = _rope_tables(t, nctx)
    ka, km, mo, qat, vat, qmt, vmt, gt, cc = _project(
        x, ctx, sc1, sh1, csc1, csh1, norm1_g[0][None, :], wn, wt, bg, cosn, sinn)

    lamv = jnp.stack([lam_q1[0], lam_k1[0], lam_q2[0], lam_k2[0]]).astype(F32)
    da = _attention(qat, ka, vat, lamv, t, lambda_init)
    hf, hb = _mlstm(qmt, km, vmt, gt, cc, t)

    return _final(x, da, hf, hb, mo, g1, sh2, sc2, g2, subln_g[0].reshape(DA_V, 1), mlstm_norm_g[0].reshape(ML_V_COLS, 1),
                  norm2_g[0][None, :], final_g[None, :], w_out[0].astype(BF16), w_fc1[0].astype(BF16),
                  w_fc2[0].astype(BF16), lambda_init)
```
